```python
import math
import jax, jax.numpy as jnp
from jax import lax
import numpy as np

D_MODEL = 2048
BATCH = 32
SEQ = 256
DEPTH = 4
DEC_BATCH = 4
DEC_SEQ = 4096
PAST_LEN = 512

GRID_W = 64
N_MIXERS = 3
N_A = (DEPTH + 2) // 3
N_B = (DEPTH + 1) // 3
N_C = DEPTH // 3

RET_HEADS = 8
RET_DK = D_MODEL // RET_HEADS
RET_DV = 2 * D_MODEL // RET_HEADS
RET_CHUNK = 128
RET_QK = RET_HEADS * RET_DK
RET_V = RET_HEADS * RET_DV
RET_IN = 2 * RET_QK + 3 * RET_V

GM_WIDTH = 2 * D_MODEL
GM_GROUPS = 8
GM_CHUNK = 128

CONV_WIDTH = 3

N_EXPERTS = 32
TOP_K = 4
D_FF = D_MODEL
SWIGLU_LIMIT = 7.0
SWIGLU_ALPHA = 1.702
MOE_BLOCK = 128

DN_ALPHA = (2.0 * DEPTH) ** 0.25
DN_BETA = (8.0 * DEPTH) ** -0.25
LN_EPS = 1e-5

kernel_name = "hybrid_flow_retention_gmlp_shortconv_moe_step"


def layer_norm(x, g=None, b=None):
    xf = x.astype(jnp.float32)
    mu = jnp.mean(xf, axis=-1, keepdims=True)
    var = jnp.mean(jnp.square(xf - mu), axis=-1, keepdims=True)
    y = (xf - mu) * lax.rsqrt(var + LN_EPS)
    if g is not None:
        y = y * g.astype(jnp.float32) + b.astype(jnp.float32)
    return y.astype(x.dtype)


def modulation(cond, ada_w, ada_b):
    m = jax.nn.silu(cond) @ ada_w + ada_b
    return jnp.split(m[..., None, :], 6, axis=-1)


def _retention_dir(q, k, v, log_gamma, s0):
    bsz, L = q.shape[0], q.shape[1]
    nc = L // RET_CHUNK

    def chunks(t):
        return t.reshape(bsz, nc, RET_CHUNK, t.shape[2], t.shape[3]).transpose(1, 0, 3, 2, 4)

    pos = jnp.arange(RET_CHUNK, dtype=jnp.float32)
    lg = log_gamma[:, None]
    rel = pos[:, None] - pos[None, :]
    dmask = jnp.where(rel >= 0, jnp.exp(lg[:, :, None] * jnp.maximum(rel, 0.0)), 0.0)
    xi = jnp.exp(lg * (pos + 1.0))
    zeta = jnp.exp(lg * (RET_CHUNK - 1.0 - pos))
    g_chunk = jnp.exp(log_gamma * RET_CHUNK)

    def step(s, qkv):
        qc, kc, vc = qkv
        scores = jnp.einsum('bhid,bhjd->bhij', qc, kc) * dmask
        o = (jnp.einsum('bhij,bhje->bhie', scores, vc)
             + jnp.einsum('bhid,bhde->bhie', qc, s) * xi[:, :, None])
        s = g_chunk[:, None, None] * s + jnp.einsum('bhjd,bhje->bhde', kc * zeta[:, :, None], vc)
        return s, o

    s_fin, o = lax.scan(step, s0, (chunks(q), chunks(k), chunks(v)))
    o = o.transpose(1, 0, 3, 2, 4).reshape(bsz, L, v.shape[2], v.shape[3])
    return o, s_fin


def retention_mixer(h, w_in, w_out, decay_param, s0_f, s0_b):
    bsz, L, _ = h.shape
    q, k, v, g_f, g_b = jnp.split(h @ w_in, [RET_QK, 2 * RET_QK, 2 * RET_QK + RET_V,
                                             2 * RET_QK + 2 * RET_V], axis=-1)
    q = q.reshape(bsz, L, RET_HEADS, RET_DK).astype(jnp.float32)
    k = k.reshape(bsz, L, RET_HEADS, RET_DK).astype(jnp.float32) * (RET_DK ** -0.5)
    v = v.reshape(bsz, L, RET_HEADS, RET_DV).astype(jnp.float32)
    log_gamma = -jnp.exp(decay_param.astype(jnp.float32))
    o_f, s_f = _retention_dir(q, k, v, log_gamma[0], s0_f.astype(jnp.float32))
    o_b, s_b = _retention_dir(q[:, ::-1], k[:, ::-1], v[:, ::-1], log_gamma[1],
                              s0_b.astype(jnp.float32))
    o_b = o_b[:, ::-1]
    g_f = jax.nn.silu(g_f).reshape(bsz, L, RET_HEADS, RET_DV)
    g_b = jax.nn.silu(g_b).reshape(bsz, L, RET_HEADS, RET_DV)
    y = g_f * layer_norm(o_f).astype(h.dtype) + g_b * layer_norm(o_b).astype(h.dtype)
    return y.reshape(bsz, L, RET_V) @ w_out, s_f.astype(h.dtype), s_b.astype(h.dtype)


def chunk_gmlp_mixer(h, w_in, b_in, ln_g, ln_b, w_s, b_s, w_out):
    bsz, L, _ = h.shape
    z = jax.nn.gelu(h @ w_in + b_in, approximate=False)
    u, v = jnp.split(z, 2, axis=-1)
    v = layer_norm(v, ln_g, ln_b)
    v = v.reshape(bsz, L // GM_CHUNK, GM_CHUNK, GM_GROUPS, GM_WIDTH // GM_GROUPS)
    mixed = jnp.einsum('gij,bnjgc->bnigc', w_s, v) + b_s.T[None, None, :, :, None]
    return (u * mixed.reshape(bsz, L, GM_WIDTH)) @ w_out


def _conv3(t, w):
    n = t.shape[-2]
    tp = jnp.pad(t, [(0, 0)] * (t.ndim - 2) + [(1, 1), (0, 0)])
    return tp[..., :n, :] * w[0] + t * w[1] + tp[..., 2:, :] * w[2]


def short_conv_mixer(h, w_in, conv_w, w_out, on_grid):
    bsz, L, d = h.shape
    b_gate, c_gate, hx = jnp.split(h @ w_in, 3, axis=-1)
    t = c_gate * hx
    if on_grid:
        rows = L // GRID_W
        t = _conv3(t.reshape(bsz, rows, GRID_W, d), conv_w).reshape(bsz, L, d)
    else:
        t = _conv3(t, conv_w)
    return (b_gate * t) @ w_out


def moe(x, router_w, router_b, w_gu, b_gu, w_down, b_down):
    shape = x.shape
    xt = x.reshape(-1, shape[-1])
    T = xt.shape[0]
    logits = (xt @ router_w + router_b).astype(jnp.float32)
    top_val, top_idx = lax.top_k(logits, TOP_K)
    gates = jax.nn.softmax(top_val, axis=-1)
    A = T * TOP_K
    e_flat = top_idx.reshape(A).astype(jnp.int32)
    tok_flat = jnp.arange(A, dtype=jnp.int32) // TOP_K
    g_flat = gates.reshape(A)
    order = jnp.argsort(e_flat)
    e_sorted = e_flat[order]
    counts = jnp.bincount(e_flat, length=N_EXPERTS).astype(jnp.int32)
    starts = jnp.cumsum(counts) - counts
    padded = (counts + MOE_BLOCK - 1) // MOE_BLOCK * MOE_BLOCK
    pad_ends = jnp.cumsum(padded)
    pad_starts = pad_ends - padded
    rank = jnp.arange(A, dtype=jnp.int32) - starts[e_sorted]
    dest = pad_starts[e_sorted] + rank
    n_blocks = -(-(A + N_EXPERTS * (MOE_BLOCK - 1)) // MOE_BLOCK)
    R = n_blocks * MOE_BLOCK
    row_tok = jnp.full((R,), T, jnp.int32).at[dest].set(tok_flat[order])
    row_gate = jnp.zeros((R,), jnp.float32).at[dest].set(g_flat[order])
    blk_start = jnp.arange(n_blocks, dtype=jnp.int32) * MOE_BLOCK
    blk_expert = jnp.minimum(jnp.searchsorted(pad_ends, blk_start, side='right'), N_EXPERTS - 1)
    x_pad = jnp.concatenate([xt, jnp.zeros((1, shape[-1]), xt.dtype)], axis=0)

    def block_fn(args):
        toks, e = args
        xb = x_pad[toks]
        gu = xb @ w_gu[e] + b_gu[e]
        gate, up = jnp.split(gu, 2, axis=-1)
        gate = jnp.minimum(gate, SWIGLU_LIMIT)
        up = jnp.clip(up, -SWIGLU_LIMIT, SWIGLU_LIMIT)
        act = (up + 1.0) * (gate * jax.nn.sigmoid(SWIGLU_ALPHA * gate))
        return act @ w_down[e] + b_down[e]

    y = lax.map(block_fn, (row_tok.reshape(n_blocks, MOE_BLOCK), blk_expert))
    y = y.reshape(R, shape[-1]) * row_gate[:, None].astype(y.dtype)
    out = jnp.zeros((T + 1, shape[-1]), y.dtype).at[row_tok].add(y)[:T]
    return out.reshape(shape)


def setup_inputs(seed: int = 0) -> dict:
    key = jax.random.key(seed)
    ks = iter(jax.random.split(key, 40))

    def nrm(shape, scale):
        return jax.random.normal(next(ks), shape, jnp.float32) * scale

    D = D_MODEL
    base_decay = jnp.log(-jnp.log1p(-(2.0 ** (-5.0 - jnp.arange(RET_HEADS, dtype=jnp.float32)))))
    return {
        "x_prompt": nrm((BATCH, SEQ, D), 1.0),
        "x_sample": nrm((DEC_BATCH, DEC_SEQ, D), 1.0),
        "state_ret": nrm((DEC_BATCH, N_A, 2, RET_HEADS, RET_DK, RET_DV), 0.05),
        "c": nrm((DEC_BATCH, D), 1.0),
        "c_ctx": nrm((D,), 1.0),
        "ada_w": nrm((DEPTH, D, 6 * D), D ** -0.5),
        "ada_b": nrm((DEPTH, 6 * D), 0.02),
        "ln_g": 1.0 + nrm((DEPTH, 2, D), 0.02),
        "ln_b": nrm((DEPTH, 2, D), 0.02),
        "ret_w_in": nrm((N_A, D, RET_IN), D ** -0.5),
        "ret_w_out": nrm((N_A, RET_V, D), RET_V ** -0.5 * DN_BETA),
        "ret_decay": jnp.broadcast_to(base_decay, (N_A, 2, RET_HEADS)) + nrm((N_A, 2, RET_HEADS), 0.05),
        "gm_w_in": nrm((N_B, D, 2 * GM_WIDTH), D ** -0.5),
        "gm_b_in": nrm((N_B, 2 * GM_WIDTH), 0.02),
        "gm_ln_g": 1.0 + nrm((N_B, GM_WIDTH), 0.02),
        "gm_ln_b": nrm((N_B, GM_WIDTH), 0.02),
        "gm_w_s": nrm((N_B, GM_GROUPS, GM_CHUNK, GM_CHUNK), GM_CHUNK ** -0.5),
        "gm_b_s": 1.0 + nrm((N_B, GM_GROUPS, GM_CHUNK), 0.02),
        "gm_w_out": nrm((N_B, GM_WIDTH, D), GM_WIDTH ** -0.5 * DN_BETA),
        "sc_w_in": nrm((N_C, D, 3 * D), D ** -0.5),
        "sc_conv": nrm((N_C, CONV_WIDTH, D), CONV_WIDTH ** -0.5),
        "sc_w_out": nrm((N_C, D, D), D ** -0.5 * DN_BETA),
        "moe_router_w": nrm((DEPTH, D, N_EXPERTS), D ** -0.5),
        "moe_router_b": nrm((DEPTH, N_EXPERTS), 0.01),
        "moe_w_gu": nrm((DEPTH, N_EXPERTS, D, 2 * D_FF), D ** -0.5),
        "moe_b_gu": nrm((DEPTH, N_EXPERTS, 2 * D_FF), 0.02),
        "moe_w_down": nrm((DEPTH, N_EXPERTS, D_FF, D), D_FF ** -0.5 * DN_BETA),
        "moe_b_down": nrm((DEPTH, N_EXPERTS, D), 0.02),
    }


def reference(x_prompt, x_sample, state_ret, c, c_ctx, ada_w, ada_b, ln_g, ln_b,
              ret_w_in, ret_w_out, ret_decay, gm_w_in, gm_b_in, gm_ln_g, gm_ln_b, gm_w_s,
              gm_b_s, gm_w_out, sc_w_in, sc_conv, sc_w_out, moe_router_w, moe_router_b,
              moe_w_gu, moe_b_gu, moe_w_down, moe_b_down):
    xc, xl = x_prompt, x_sample
    ctx_states = []
    for i in range(DEPTH):
        kind, j = i % N_MIXERS, i // N_MIXERS
        mc = modulation(c_ctx, ada_w[i], ada_b[i])
        ml = modulation(c, ada_w[i], ada_b[i])
        hc = xc * (1.0 + mc[1]) + mc[0]
        hl = xl * (1.0 + ml[1]) + ml[0]
        if kind == 0:
            s_zero = jnp.zeros((xc.shape[0], RET_HEADS, RET_DK, RET_DV), jnp.float32)
            oc, sf, sb = retention_mixer(hc, ret_w_in[j], ret_w_out[j], ret_decay[j], s_zero, s_zero)
            ctx_states.append(jnp.stack([sf, sb], axis=1))
            ol, _, _ = retention_mixer(hl, ret_w_in[j], ret_w_out[j], ret_decay[j],
                                       state_ret[:, j, 0], state_ret[:, j, 1])
        elif kind == 1:
            oc = chunk_gmlp_mixer(hc, gm_w_in[j], gm_b_in[j], gm_ln_g[j], gm_ln_b[j], gm_w_s[j],
                                  gm_b_s[j], gm_w_out[j])
            ol = chunk_gmlp_mixer(hl, gm_w_in[j], gm_b_in[j], gm_ln_g[j], gm_ln_b[j], gm_w_s[j],
                                  gm_b_s[j], gm_w_out[j])
        else:
            oc = short_conv_mixer(hc, sc_w_in[j], sc_conv[j], sc_w_out[j], on_grid=False)
            ol = short_conv_mixer(hl, sc_w_in[j], sc_conv[j], sc_w_out[j], on_grid=True)
        xc = layer_norm(DN_ALPHA * xc + mc[2] * oc, ln_g[i, 0], ln_b[i, 0])
        xl = layer_norm(DN_ALPHA * xl + ml[2] * ol, ln_g[i, 0], ln_b[i, 0])
        hc = xc * (1.0 + mc[4]) + mc[3]
        hl = xl * (1.0 + ml[4]) + ml[3]
        fc = moe(hc, moe_router_w[i], moe_router_b[i], moe_w_gu[i], moe_b_gu[i], moe_w_down[i], moe_b_down[i])
        fl = moe(hl, moe_router_w[i], moe_router_b[i], moe_w_gu[i], moe_b_gu[i], moe_w_down[i], moe_b_down[i])
        xc = layer_norm(DN_ALPHA * xc + mc[5] * fc, ln_g[i, 1], ln_b[i, 1])
        xl = layer_norm(DN_ALPHA * xl + ml[5] * fl, ln_g[i, 1], ln_b[i, 1])
    new_state_ret = jnp.stack(ctx_states, axis=1)
    return (xc, xl, new_state_ret)
```

```python
import functools
import math

import jax
import jax.numpy as jnp
from jax import lax
from jax.experimental import pallas as pl
from jax.experimental.pallas import tpu as pltpu

F32 = jnp.float32
BF16 = jnp.bfloat16
I32 = jnp.int32

N_MIXERS = 3
RET_HEADS = 8
RET_CHUNK = 128
GM_GROUPS = 8
GM_CHUNK = 128
GRID_W = 64
N_EXPERTS = 32
TOP_K = 4
SWIGLU_LIMIT = 7.0
SWIGLU_ALPHA = 1.702
LN_EPS = 1e-5

MOD_ROWS = 8
VMEM_LIMIT_BYTES = 52 * 1024 * 1024
MOE_ROWS = 256

NT_DIMS = (((1,), (1,)), ((), ()))
TN_DIMS = (((0,), (0,)), ((), ()))


def _tile(n, pref):
    t = min(n, pref)
    while n % t:
        t //= 2
    return t


def _row_tile(tc, ll, pref):
    return _tile(math.gcd(tc, ll), pref)


def _params(*sem):
    return pltpu.CompilerParams(dimension_semantics=sem, vmem_limit_bytes=VMEM_LIMIT_BYTES)


def _group(i, tm, tc, ll):
    row0 = i * tm
    return jnp.where(row0 < tc, 0, 1 + (row0 - tc) // ll)


def _norm(r):
    mu = jnp.mean(r, axis=-1, keepdims=True)
    d = r - mu
    var = jnp.mean(d * d, axis=-1, keepdims=True)
    return d * lax.rsqrt(var + LN_EPS)


def _silu(x):
    return x * jax.nn.sigmoid(x)


def _mod_kernel(c_ref, w_ref, b_ref, o_ref):
    s = _silu(c_ref[...]).astype(BF16)
    o_ref[0] = jnp.dot(s, w_ref[0].astype(BF16), preferred_element_type=F32) + b_ref[0]


def _modulation(cond, ada_w, ada_b):
    L, D, N = ada_w.shape
    tn = _tile(N, 1024)
    return pl.pallas_call(
        _mod_kernel,
        grid=(L, N // tn),
        in_specs=[pl.BlockSpec((MOD_ROWS, D), lambda l, j: (0, 0)),
                  pl.BlockSpec((1, D, tn), lambda l, j: (l, 0, j)),
                  pl.BlockSpec((1, 1, tn), lambda l, j: (l, 0, j))],
        out_specs=pl.BlockSpec((1, MOD_ROWS, tn), lambda l, j: (l, 0, j)),
        out_shape=jax.ShapeDtypeStruct((L, MOD_ROWS, N), F32),
        compiler_params=_params("parallel", "parallel"),
        name="adaln_modulation",
    )(cond, ada_w, ada_b.reshape(L, 1, N))


def _mm_mod_kernel(*refs, tm, tc, ll, has_bias, gelu):
    if has_bias:
        x_ref, sh_ref, sc_ref, w_ref, b_ref, o_ref, hb_ref = refs
    else:
        x_ref, sh_ref, sc_ref, w_ref, o_ref, hb_ref = refs
    i, j = pl.program_id(0), pl.program_id(1)

    @pl.when(j == 0)
    def _():
        g = _group(i, tm, tc, ll)
        h = x_ref[...] * (1.0 + sc_ref[pl.ds(g, 1), :]) + sh_ref[pl.ds(g, 1), :]
        hb_ref[...] = h.astype(BF16)

    acc = jnp.dot(hb_ref[...], w_ref[...], preferred_element_type=F32)
    if has_bias:
        acc = acc + b_ref[...]
    if gelu:
        acc = 0.5 * acc * (1.0 + lax.erf(acc * (2.0 ** -0.5)))
    o_ref[...] = acc.astype(o_ref.dtype)


def _mm_mod(x, mod, shift_chunk, w, bias, *, tc, ll, gelu=False):
    T, D = x.shape
    N = w.shape[1]
    tm = _row_tile(tc, ll, 1024)
    tn = _tile(N, 512)
    in_specs = [pl.BlockSpec((tm, D), lambda i, j: (i, 0)),
                pl.BlockSpec((MOD_ROWS, D), lambda i, j: (0, shift_chunk)),
                pl.BlockSpec((MOD_ROWS, D), lambda i, j: (0, shift_chunk + 1)),
                pl.BlockSpec((D, tn), lambda i, j: (0, j))]
    args = [x, mod, mod, w]
    if bias is not None:
        in_specs.append(pl.BlockSpec((1, tn), lambda i, j: (0, j)))
        args.append(bias.reshape(1, N))
    return pl.pallas_call(
        functools.partial(_mm_mod_kernel, tm=tm, tc=tc, ll=ll, has_bias=bias is not None, gelu=gelu),
        grid=(T // tm, N // tn),
        in_specs=in_specs,
        out_specs=pl.BlockSpec((tm, tn), lambda i, j: (i, j)),
        out_shape=jax.ShapeDtypeStruct((T, N), BF16),
        scratch_shapes=[pltpu.VMEM((tm, D), BF16)],
        compiler_params=_params("parallel", "arbitrary"),
        name="modulate_in_proj",
    )(*args)


def _out_ln_kernel(*refs, n_y, nk, tm, tc, ll, alpha):
    y_refs = refs[:n_y]
    w_ref, x_ref, gate_ref, lng_ref, lnb_ref, o_ref, acc_ref = refs[n_y:]
    i, k = pl.program_id(0), pl.program_id(1)

    @pl.when(k == 0)
    def _():
        acc_ref[...] = jnp.zeros_like(acc_ref)

    if n_y == 1:
        y = y_refs[0][...]
    else:
        y = (y_refs[0][...].astype(F32) + y_refs[1][...].astype(F32)).astype(BF16)
    acc_ref[...] += jnp.dot(y, w_ref[...], preferred_element_type=F32)

    @pl.when(k == nk - 1)
    def _():
        g = _group(i, tm, tc, ll)
        r = alpha * x_ref[...] + gate_ref[pl.ds(g, 1), :] * acc_ref[...]
        o_ref[...] = _norm(r) * lng_ref[...] + lnb_ref[...]


def _out_ln(ys, w, x, mod, gate_chunk, ln_g, ln_b, *, tc, ll, alpha):
    T, D = x.shape
    K = w.shape[0]
    tm = _row_tile(tc, ll, 512)
    tk = _tile(K, 1024)
    nk = K // tk
    in_specs = [pl.BlockSpec((tm, tk), lambda i, k: (i, k)) for _ in ys]
    in_specs += [pl.BlockSpec((tk, D), lambda i, k: (k, 0)),
                 pl.BlockSpec((tm, D), lambda i, k: (i, 0)),
                 pl.BlockSpec((MOD_ROWS, D), lambda i, k: (0, gate_chunk)),
                 pl.BlockSpec((1, D), lambda i, k: (0, 0)),
                 pl.BlockSpec((1, D), lambda i, k: (0, 0))]
    return pl.pallas_call(
        functools.partial(_out_ln_kernel, n_y=len(ys), nk=nk, tm=tm, tc=tc, ll=ll, alpha=alpha),
        grid=(T // tm, nk),
        in_specs=in_specs,
        out_specs=pl.BlockSpec((tm, D), lambda i, k: (i, 0)),
        out_shape=jax.ShapeDtypeStruct((T, D), F32),
        scratch_shapes=[pltpu.VMEM((tm, D), F32)],
        compiler_params=_params("parallel", "arbitrary"),
        name="out_proj_residual_ln",
    )(*ys, w, x, mod, ln_g.reshape(1, D), ln_b.reshape(1, D))


def _ret_item(n, lay):
    ncx, nc_c, nc_l = lay
    is_ctx = n < ncx
    m = jnp.where(is_ctx, n, n - ncx)
    seq = jnp.where(is_ctx, m // nc_c, m // nc_l)
    c = jnp.where(is_ctx, m % nc_c, m % nc_l)
    nc = jnp.where(is_ctx, nc_c, nc_l)
    return is_ctx, seq, c, nc, n - c


def _ret_kernel(*refs, n_prev, lay, scale):
    dec_ref, qf, kf, vf, gf, qb, kb, vb, gb, s0_ref = refs[:10]
    prev_refs = refs[10:10 + n_prev]
    yf_ref, yb_ref, out_ref, st_ref = refs[10 + n_prev:]
    h = pl.program_id(0)
    is_ctx, _, c, nc, _ = _ret_item(pl.program_id(1), lay)
    C = RET_CHUNK

    @pl.when(c == 0)
    def _():
        st_ref[...] = jnp.where(is_ctx, 0.0, s0_ref[...])

    ri = lax.broadcasted_iota(I32, (C, C), 0)
    ci = lax.broadcasted_iota(I32, (C, C), 1)
    row = lax.broadcasted_iota(I32, (C, 1), 0).astype(F32)
    dirs = ((qf, kf, vf, gf, yf_ref), (qb, kb, vb, gb, yb_ref))
    for d, (q_ref, k_ref, v_ref, g_ref, y_ref) in enumerate(dirs):
        lg = -jnp.exp(jnp.full((1, 1), dec_ref[d, h], F32))
        rel = ri - ci if d == 0 else ci - ri
        dmask = jnp.where(rel >= 0, jnp.exp(lg * jnp.maximum(rel, 0).astype(F32)), 0.0) * scale
        pos = row if d == 0 else (C - 1.0) - row
        xi = jnp.exp(lg * (pos + 1.0))
        zeta = jnp.exp(lg * ((C - 1.0) - pos)) * scale
        q, k, v = q_ref[...], k_ref[...], v_ref[...]
        s = st_ref[d]
        scores = lax.dot_general(q, k, NT_DIMS, preferred_element_type=F32) * dmask
        o = (jnp.dot(scores.astype(BF16), v, preferred_element_type=F32)
             + jnp.dot(q, s.astype(BF16), preferred_element_type=F32) * xi)
        kz = (k.astype(F32) * zeta).astype(BF16)
        st_ref[d] = jnp.exp(lg * float(C)) * s + lax.dot_general(kz, v, TN_DIMS, preferred_element_type=F32)
        y_ref[...] = (_silu(g_ref[...].astype(F32)) * _norm(o)).astype(y_ref.dtype)

    @pl.when(jnp.logical_and(is_ctx, c == nc - 1))
    def _():
        for p, prev_ref in enumerate(prev_refs):
            out_ref[p] = prev_ref[...]
        out_ref[n_prev] = st_ref[...]


def _retention(a, decay, s0, s0_layer, prev_states, *, n_ctx_seq, ctx_len, lat_len):
    T = a.shape[0]
    H = RET_HEADS
    DK, DV = s0.shape[-2], s0.shape[-1]
    C = RET_CHUNK
    lay = (n_ctx_seq * ctx_len // C, ctx_len // C, lat_len // C)
    n_prev = 0 if prev_states is None else prev_states.shape[1]

    def fwd(n):
        return n

    def bwd(n):
        _, _, c, nc, first = _ret_item(n, lay)
        return first + (nc - 1 - c)

    def lat_seq(n):
        is_ctx, seq, _, _, _ = _ret_item(n, lay)
        return jnp.where(is_ctx, 0, seq)

    def ctx_seq(n):
        is_ctx, seq, _, _, _ = _ret_item(n, lay)
        return jnp.where(is_ctx, seq, n_ctx_seq - 1)

    def col_specs(rowf):
        return [pl.BlockSpec((C, DK), lambda h, n: (rowf(n), h)),
                pl.BlockSpec((C, DK), lambda h, n: (rowf(n), H + h)),
                pl.BlockSpec((C, DV), lambda h, n: (rowf(n), H + h))]

    def state_spec(layers, seqf):
        return pl.BlockSpec((None, layers, 2, None, DK, DV), lambda h, n: (seqf(n), 0, 0, h, 0, 0))

    in_specs = [pl.BlockSpec(memory_space=pltpu.SMEM)]
    in_specs += col_specs(fwd) + [pl.BlockSpec((C, DV), lambda h, n: (fwd(n), 2 * H + h))]
    in_specs += col_specs(bwd) + [pl.BlockSpec((C, DV), lambda h, n: (bwd(n), 3 * H + h))]
    in_specs.append(pl.BlockSpec((None, None, 2, None, DK, DV), lambda h, n: (lat_seq(n), s0_layer, 0, h, 0, 0)))
    args = [decay] + [a] * 8 + [s0]
    if n_prev:
        in_specs.append(state_spec(n_prev, ctx_seq))
        args.append(prev_states)

    def kern(*refs):
        if n_prev:
            refs = refs[:10] + tuple(refs[10].at[p] for p in range(n_prev)) + refs[11:]
        _ret_kernel(*refs, n_prev=n_prev, lay=lay, scale=float(DK) ** -0.5)

    V = H * DV
    return pl.pallas_call(
        kern,
        grid=(H, T // C),
        in_specs=in_specs,
        out_specs=[pl.BlockSpec((C, DV), lambda h, n: (fwd(n), h)),
                   pl.BlockSpec((C, DV), lambda h, n: (bwd(n), h)),
                   state_spec(n_prev + 1, ctx_seq)],
        out_shape=[jax.ShapeDtypeStruct((T, V), BF16), jax.ShapeDtypeStruct((T, V), BF16),
                   jax.ShapeDtypeStruct((n_ctx_seq, n_prev + 1, 2, H, DK, DV), F32)],
        scratch_shapes=[pltpu.VMEM((2, DK, DV), F32)],
        compiler_params=_params("parallel", "arbitrary"),
        name="retention_core",
    )(*args)


def _gmlp_kernel(u_ref, v_ref, lng_ref, lnb_ref, ws_ref, bs_ref, o_ref):
    vn = (_norm(v_ref[...].astype(F32)) * lng_ref[...] + lnb_ref[...]).astype(BF16)
    gw = vn.shape[1] // GM_GROUPS
    for g in range(GM_GROUPS):
        cols = slice(g * gw, (g + 1) * gw)
        mixed = jnp.dot(ws_ref[g], vn[:, cols], preferred_element_type=F32) + bs_ref[:, g:g + 1]
        o_ref[:, cols] = (u_ref[:, cols].astype(F32) * mixed).astype(o_ref.dtype)


def _gmlp_core(z, ln_g, ln_b, w_s, b_s):
    T, W2 = z.shape
    W = W2 // 2
    C = GM_CHUNK
    return pl.pallas_call(
        _gmlp_kernel,
        grid=(T // C,),
        in_specs=[pl.BlockSpec((C, W), lambda n: (n, 0)),
                  pl.BlockSpec((C, W), lambda n: (n, 1)),
                  pl.BlockSpec((1, W), lambda n: (0, 0)),
                  pl.BlockSpec((1, W), lambda n: (0, 0)),
                  pl.BlockSpec((GM_GROUPS, C, C), lambda n: (0, 0, 0)),
                  pl.BlockSpec((C, GM_GROUPS), lambda n: (0, 0))],
        out_specs=pl.BlockSpec((C, W), lambda n: (n, 0)),
        out_shape=jax.ShapeDtypeStruct((T, W), BF16),
        compiler_params=_params("parallel"),
        name="gmlp_spatial_gate",
    )(z, z, ln_g.reshape(1, W), ln_b.reshape(1, W), w_s.astype(BF16), b_s.T)


def _sconv_kernel(b_ref, c_ref, x_ref, w_ref, o_ref, *, tm, tc, p_ctx, p_lat):
    i = pl.program_id(0)
    t = c_ref[...].astype(F32) * x_ref[...].astype(F32)
    period = jnp.where(i * tm < tc, p_ctx, p_lat)
    pos = lax.rem(lax.broadcasted_iota(I32, (tm, 1), 0), period)
    t_prev = jnp.where(pos == 0, 0.0, pltpu.roll(t, 1, 0))
    t_next = jnp.where(pos == period - 1, 0.0, pltpu.roll(t, tm - 1, 0))
    conv = t_prev * w_ref[0:1, :] + t * w_ref[1:2, :] + t_next * w_ref[2:3, :]
    o_ref[...] = (b_ref[...].astype(F32) * conv).astype(o_ref.dtype)


def _sconv_core(a, conv_w, *, tc, ll, p_ctx, p_lat):
    T, D3 = a.shape
    D = D3 // 3
    tm = _row_tile(tc, ll, 512)
    assert tm % p_ctx == 0 and tm % p_lat == 0
    tn = _tile(D, 512)
    nj = D // tn
    return pl.pallas_call(
        functools.partial(_sconv_kernel, tm=tm, tc=tc, p_ctx=p_ctx, p_lat=p_lat),
        grid=(T // tm, nj),
        in_specs=[pl.BlockSpec((tm, tn), lambda i, j: (i, j)),
                  pl.BlockSpec((tm, tn), lambda i, j: (i, nj + j)),
                  pl.BlockSpec((tm, tn), lambda i, j: (i, 2 * nj + j)),
                  pl.BlockSpec((3, tn), lambda i, j: (0, j))],
        out_specs=pl.BlockSpec((tm, tn), lambda i, j: (i, j)),
        out_shape=jax.ShapeDtypeStruct((T, D), BF16),
        compiler_params=_params("parallel", "parallel"),
        name="short_conv_gate",
    )(a, a, a, conv_w)


def _router_kernel(x_ref, sh_ref, sc_ref, wt_ref, b_ref, h_ref, idx_ref, gate_ref, rank_ref, cnt_ref,
                   carry_ref, *, tm, tc, ll):
    i = pl.program_id(0)
    E = N_EXPERTS

    @pl.when(i == 0)
    def _():
        carry_ref[...] = jnp.zeros_like(carry_ref)

    g = _group(i, tm, tc, ll)
    h = x_ref[...] * (1.0 + sc_ref[pl.ds(g, 1), :]) + sh_ref[pl.ds(g, 1), :]
    h_ref[...] = h
    hh = h.astype(BF16)
    hl = (h - hh.astype(F32)).astype(BF16)
    w = wt_ref[...]
    wh = w.astype(BF16)
    wl = (w - wh.astype(F32)).astype(BF16)
    logits = (lax.dot_general(wh, hh, NT_DIMS, preferred_element_type=F32)
              + lax.dot_general(wh, hl, NT_DIMS, preferred_element_type=F32)
              + lax.dot_general(wl, hh, NT_DIMS, preferred_element_type=F32)) + b_ref[...]
    e_iota = lax.broadcasted_iota(I32, (E, tm), 0)
    earlier = (lax.broadcasted_iota(I32, (tm, tm), 0) < lax.broadcasted_iota(I32, (tm, tm), 1)).astype(BF16)
    base = carry_ref[...]
    vals = logits
    tops = []
    for k in range(TOP_K):
        m = jnp.max(vals, axis=0, keepdims=True)
        idx = jnp.min(jnp.where(vals == m, e_iota, E), axis=0, keepdims=True)
        hit = e_iota == idx
        onehot = jnp.where(hit, 1.0, 0.0)
        before = jnp.dot(onehot.astype(BF16), earlier, preferred_element_type=F32)
        rank = jnp.sum(onehot * (base + before), axis=0, keepdims=True)
        base = base + jnp.sum(onehot, axis=1, keepdims=True)
        idx_ref[k:k + 1, :] = idx
        rank_ref[k:k + 1, :] = rank.astype(I32)
        tops.append(m)
        vals = jnp.where(hit, -jnp.inf, vals)
    carry_ref[...] = base
    cnt_ref[...] = base
    exps = [jnp.exp(t - tops[0]) for t in tops]
    den = exps[0]
    for e in exps[1:]:
        den = den + e
    for k in range(TOP_K):
        gate_ref[k:k + 1, :] = exps[k] / den


def _router(x, mod, shift_chunk, router_w, router_b, *, tc, ll):
    T, D = x.shape
    E = N_EXPERTS
    tm = _row_tile(tc, ll, 256)
    row = lambda i: (0, i)
    return pl.pallas_call(
        functools.partial(_router_kernel, tm=tm, tc=tc, ll=ll),
        grid=(T // tm,),
        in_specs=[pl.BlockSpec((tm, D), lambda i: (i, 0)),
                  pl.BlockSpec((MOD_ROWS, D), lambda i: (0, shift_chunk)),
                  pl.BlockSpec((MOD_ROWS, D), lambda i: (0, shift_chunk + 1)),
                  pl.BlockSpec((E, D), lambda i: (0, 0)),
                  pl.BlockSpec((E, 1), lambda i: (0, 0))],
        out_specs=[pl.BlockSpec((tm, D), lambda i: (i, 0)),
                   pl.BlockSpec((TOP_K, tm), row),
                   pl.BlockSpec((TOP_K, tm), row),
                   pl.BlockSpec((TOP_K, tm), row),
                   pl.BlockSpec((E, 1), lambda i: (0, 0))],
        out_shape=[jax.ShapeDtypeStruct((T, D), F32),
                   jax.ShapeDtypeStruct((TOP_K, T), I32),
                   jax.ShapeDtypeStruct((TOP_K, T), F32),
                   jax.ShapeDtypeStruct((TOP_K, T), I32),
                   jax.ShapeDtypeStruct((E, 1), F32)],
        scratch_shapes=[pltpu.VMEM((E, 1), F32)],
        compiler_params=_params("arbitrary"),
        name="moe_router",
    )(x, mod, mod, router_w.T, router_b.reshape(E, 1))


def _dispatch_kernel(dest_ref, h_ref, xs_in_ref, xs_ref, sem):
    del xs_in_ref
    tm = h_ref.shape[0]

    def row_copy(r, k):
        return pltpu.make_async_copy(h_ref.at[pl.ds(r, 1)], xs_ref.at[pl.ds(dest_ref[k, r], 1)], sem)

    def issue(r, carry):
        for k in range(TOP_K):
            row_copy(r, k).start()
        return carry

    lax.fori_loop(0, tm, issue, 0)
    for k in range(TOP_K):
        pltpu.make_async_copy(h_ref, xs_ref.at[pl.ds(0, tm)], sem).wait()


def _dispatch(h, dest, n_rows):
    T, D = h.shape
    tm = _tile(T, 256)
    xs0 = jnp.zeros((n_rows, D), h.dtype)
    return pl.pallas_call(
        _dispatch_kernel,
        grid=(T // tm,),
        in_specs=[pl.BlockSpec((TOP_K, tm), lambda i: (0, i), memory_space=pltpu.SMEM),
                  pl.BlockSpec((tm, D), lambda i: (i, 0)),
                  pl.BlockSpec(memory_space=pl.ANY)],
        out_specs=pl.BlockSpec(memory_space=pl.ANY),
        out_shape=jax.ShapeDtypeStruct((n_rows, D), h.dtype),
        scratch_shapes=[pltpu.SemaphoreType.DMA(())],
        input_output_aliases={2: 0},
        compiler_params=_params("arbitrary"),
        name="moe_dispatch",
    )(dest, h, xs0)


def _expert_changed(be_ref, i):
    return jnp.logical_or(i == 0, be_ref[i] != be_ref[jnp.maximum(i - 1, 0)])


def _gate_up_kernel(be_ref, nu_ref, x_ref, wg_ref, wu_ref, bg_ref, bu_ref, o_ref, wgb_ref, wub_ref):
    i = pl.program_id(1)

    @pl.when(_expert_changed(be_ref, i))
    def _():
        wgb_ref[...] = wg_ref[...].astype(BF16)
        wub_ref[...] = wu_ref[...].astype(BF16)

    @pl.when(i < nu_ref[0])
    def _():
        x = x_ref[...].astype(BF16)
        gate = jnp.dot(x, wgb_ref[...], preferred_element_type=F32) + bg_ref[...]
        up = jnp.dot(x, wub_ref[...], preferred_element_type=F32) + bu_ref[...]
        gate = jnp.minimum(gate, SWIGLU_LIMIT)
        up = jnp.clip(up, -SWIGLU_LIMIT, SWIGLU_LIMIT)
        act = (up + 1.0) * (gate * jax.nn.sigmoid(SWIGLU_ALPHA * gate))
        o_ref[...] = act.astype(o_ref.dtype)

    @pl.when(i >= nu_ref[0])
    def _():
        o_ref[...] = jnp.zeros_like(o_ref)


def _gate_up(xs, blk_expert, n_used, w_gu, b_gu, layer):
    R, D = xs.shape
    F2 = w_gu.shape[-1]
    F = F2 // 2
    tm = MOE_ROWS
    tf = _tile(F, 512)
    nf = F // tf
    nb = R // tm
    last = lambda i, nu: jnp.minimum(i, nu[0] - 1)
    return pl.pallas_call(
        _gate_up_kernel,
        grid_spec=pltpu.PrefetchScalarGridSpec(
            num_scalar_prefetch=2,
            grid=(nf, nb),
            in_specs=[pl.BlockSpec((tm, D), lambda j, i, be, nu: (last(i, nu), 0)),
                      pl.BlockSpec((None, None, D, tf), lambda j, i, be, nu: (layer, be[i], 0, j)),
                      pl.BlockSpec((None, None, D, tf), lambda j, i, be, nu: (layer, be[i], 0, nf + j)),
                      pl.BlockSpec((None, None, 1, tf), lambda j, i, be, nu: (layer, be[i], 0, j)),
                      pl.BlockSpec((None, None, 1, tf), lambda j, i, be, nu: (layer, be[i], 0, nf + j))],
            out_specs=pl.BlockSpec((tm, tf), lambda j, i, be, nu: (i, j)),
            scratch_shapes=[pltpu.VMEM((D, tf), BF16), pltpu.VMEM((D, tf), BF16)]),
        out_shape=jax.ShapeDtypeStruct((R, F), BF16),
        compiler_params=_params("arbitrary", "arbitrary"),
        name="moe_gate_up",
    )(blk_expert, n_used, xs, w_gu, w_gu, b_gu, b_gu)


def _down_kernel(be_ref, nu_ref, a_ref, w_ref, b_ref, o_ref, wb_ref):
    i = pl.program_id(1)

    @pl.when(_expert_changed(be_ref, i))
    def _():
        wb_ref[...] = w_ref[...].astype(BF16)

    @pl.when(i < nu_ref[0])
    def _():
        o_ref[...] = jnp.dot(a_ref[...], wb_ref[...], preferred_element_type=F32) + b_ref[...]

    @pl.when(i >= nu_ref[0])
    def _():
        o_ref[...] = jnp.zeros_like(o_ref)


def _down(act, blk_expert, n_used, w_down, b_down, layer):
    R, F = act.shape
    D = w_down.shape[-1]
    tm = MOE_ROWS
    tn = _tile(D, 512)
    nb = R // tm
    last = lambda i, nu: jnp.minimum(i, nu[0] - 1)
    return pl.pallas_call(
        _down_kernel,
        grid_spec=pltpu.PrefetchScalarGridSpec(
            num_scalar_prefetch=2,
            grid=(D // tn, nb),
            in_specs=[pl.BlockSpec((tm, F), lambda j, i, be, nu: (last(i, nu), 0)),
                      pl.BlockSpec((None, None, F, tn), lambda j, i, be, nu: (layer, be[i], 0, j)),
                      pl.BlockSpec((None, None, 1, tn), lambda j, i, be, nu: (layer, be[i], 0, j))],
            out_specs=pl.BlockSpec((tm, tn), lambda j, i, be, nu: (i, j)),
            scratch_shapes=[pltpu.VMEM((F, tn), BF16)]),
        out_shape=jax.ShapeDtypeStruct((R, D), F32),
        compiler_params=_params("arbitrary", "arbitrary"),
        name="moe_down",
    )(blk_expert, n_used, act, w_down, b_down)


def _combine_kernel(dest_ref, gt_ref, x_ref, gate_ref, lng_ref, lnb_ref, ys_ref, o_ref, buf_ref, sem,
                    *, tm, tc, ll, alpha):
    i = pl.program_id(0)

    def row_copy(r, k):
        return pltpu.make_async_copy(ys_ref.at[pl.ds(dest_ref[k, r], 1)], buf_ref.at[k, pl.ds(r, 1)], sem)

    def issue(r, carry):
        for k in range(TOP_K):
            row_copy(r, k).start()
        return carry

    lax.fori_loop(0, tm, issue, 0)
    for k in range(TOP_K):
        pltpu.make_async_copy(ys_ref.at[pl.ds(0, tm)], buf_ref.at[k], sem).wait()
    f = gt_ref[:, 0:1] * buf_ref[0]
    for k in range(1, TOP_K):
        f = f + gt_ref[:, k:k + 1] * buf_ref[k]
    g = _group(i, tm, tc, ll)
    r = alpha * x_ref[...] + gate_ref[pl.ds(g, 1), :] * f
    o_ref[...] = _norm(r) * lng_ref[...] + lnb_ref[...]


def _combine(ys, dest, gates_t, x, mod, gate_chunk, ln_g, ln_b, *, tc, ll, alpha):
    T, D = x.shape
    tm = _row_tile(tc, ll, 256)
    return pl.pallas_call(
        functools.partial(_combine_kernel, tm=tm, tc=tc, ll=ll, alpha=alpha),
        grid=(T // tm,),
        in_specs=[pl.BlockSpec((TOP_K, tm), lambda i: (0, i), memory_space=pltpu.SMEM),
                  pl.BlockSpec((tm, TOP_K), lambda i: (i, 0)),
                  pl.BlockSpec((tm, D), lambda i: (i, 0)),
                  pl.BlockSpec((MOD_ROWS, D), lambda i: (0, gate_chunk)),
                  pl.BlockSpec((1, D), lambda i: (0, 0)),
                  pl.BlockSpec((1, D), lambda i: (0, 0)),
                  pl.BlockSpec(memory_space=pl.ANY)],
        out_specs=pl.BlockSpec((tm, D), lambda i: (i, 0)),
        out_shape=jax.ShapeDtypeStruct((T, D), F32),
        scratch_shapes=[pltpu.VMEM((TOP_K, tm, D), F32), pltpu.SemaphoreType.DMA(())],
        compiler_params=_params("arbitrary"),
        name="moe_combine_residual_ln",
    )(dest, gates_t, x, mod, ln_g.reshape(1, D), ln_b.reshape(1, D), ys)


def _moe_block(x, mod, layer, router_w, router_b, w_gu, b_gu, w_down, b_down, ln_g, ln_b, *, tc, ll, alpha):
    T, D = x.shape
    E = N_EXPERTS
    h, idx, gates, rank, cnt = _router(x, mod, 3, router_w, router_b, tc=tc, ll=ll)
    counts = cnt[:, 0].astype(I32)
    padded = (counts + MOE_ROWS - 1) // MOE_ROWS * MOE_ROWS
    pad_ends = jnp.cumsum(padded)
    pad_starts = pad_ends - padded
    onehot = idx[:, :, None] == jnp.arange(E, dtype=I32)
    dest = jnp.sum(jnp.where(onehot, pad_starts, 0), axis=-1) + rank
    nb = pl.cdiv(T * TOP_K + E * (MOE_ROWS - 1), MOE_ROWS)
    blk_start = jnp.arange(nb, dtype=I32) * MOE_ROWS
    blk_expert = jnp.minimum(jnp.searchsorted(pad_ends, blk_start, side="right"), E - 1).astype(I32)
    n_used = (pad_ends[-1:] // MOE_ROWS).astype(I32)
    xs = _dispatch(h, dest, nb * MOE_ROWS)
    E4 = b_gu.shape
    act = _gate_up(xs, blk_expert, n_used, w_gu, b_gu.reshape(E4[0], E4[1], 1, E4[2]), layer)
    ys = _down(act, blk_expert, n_used, w_down, b_down.reshape(b_down.shape[0], E, 1, D), layer)
    return _combine(ys, dest, gates.T, x, mod, 5, ln_g, ln_b, tc=tc, ll=ll, alpha=alpha)


def kernel(x_prompt, x_sample, state_ret, c, c_ctx, ada_w, ada_b, ln_g, ln_b, ret_w_in, ret_w_out, ret_decay,
           gm_w_in, gm_b_in, gm_ln_g, gm_ln_b, gm_w_s, gm_b_s, gm_w_out, sc_w_in, sc_conv, sc_w_out,
           moe_router_w, moe_router_b, moe_w_gu, moe_b_gu, moe_w_down, moe_b_down):
    B, S, D = x_prompt.shape
    BL, SL, _ = x_sample.shape
    depth = ada_w.shape[0]
    assert 1 + BL <= MOD_ROWS
    assert S % RET_CHUNK == 0 and SL % RET_CHUNK == 0 and SL % GRID_W == 0
    tc, ll = B * S, SL
    alpha = (2.0 * depth) ** 0.25

    x = jnp.concatenate([x_prompt.reshape(tc, D), x_sample.reshape(BL * SL, D)], axis=0)
    cond = jnp.concatenate([c_ctx[None], c, jnp.zeros((MOD_ROWS - 1 - BL, D), F32)], axis=0)
    mods = _modulation(cond, ada_w, ada_b)
    states = None
    kw = dict(tc=tc, ll=ll)
    for i in range(depth):
        kind, j = i % N_MIXERS, i // N_MIXERS
        mod = mods[i]
        if kind == 0:
            a = _mm_mod(x, mod, 0, ret_w_in[j].astype(BF16), None, **kw)
            yf, yb, states = _retention(a, ret_decay[j], state_ret, j, states, n_ctx_seq=B, ctx_len=S, lat_len=SL)
            ys, w_out = [yf, yb], ret_w_out[j]
        elif kind == 1:
            z = _mm_mod(x, mod, 0, gm_w_in[j].astype(BF16), gm_b_in[j], gelu=True, **kw)
            ys, w_out = [_gmlp_core(z, gm_ln_g[j], gm_ln_b[j], gm_w_s[j], gm_b_s[j])], gm_w_out[j]
        else:
            a = _mm_mod(x, mod, 0, sc_w_in[j].astype(BF16), None, **kw)
            ys, w_out = [_sconv_core(a, sc_conv[j], p_ctx=S, p_lat=GRID_W, **kw)], sc_w_out[j]
        x = _out_ln(ys, w_out.astype(BF16), x, mod, 2, ln_g[i, 0], ln_b[i, 0], alpha=alpha, **kw)
        x = _moe_block(x, mod, i, moe_router_w[i], moe_router_b[i], moe_w_gu, moe_b_gu, moe_w_down, moe_b_down,
                       ln_g[i, 1], ln_b[i, 1], alpha=alpha, **kw)
    return (x[:tc].reshape(B, S, D), x[tc:].reshape(BL, SL, D), states)
```

```python
import functools
import math

import jax
import jax.numpy as jnp
from jax import lax
from jax.experimental import pallas as pl
from jax.experimental.pallas import tpu as pltpu

F32 = jnp.float32
BF16 = jnp.bfloat16
I32 = jnp.int32

N_MIXERS = 3
RET_HEADS = 8
RET_CHUNK = 128
GM_GROUPS = 8
GM_CHUNK = 128
GRID_W = 64
N_EXPERTS = 32
TOP_K = 4
SWIGLU_LIMIT = 7.0
SWIGLU_ALPHA = 1.702
LN_EPS = 1e-5

MOD_ROWS = 8
VMEM_LIMIT_BYTES = 52 * 1024 * 1024
MOE_ROWS = 256
RET_HEADS_PER_STEP = 2

NT_DIMS = (((1,), (1,)), ((), ()))
TN_DIMS = (((0,), (0,)), ((), ()))


def _tile(n, pref):
    t = min(n, pref)
    while n % t:
        t //= 2
    return t


def _row_tile(tc, ll, pref):
    return _tile(math.gcd(tc, ll), pref)


def _params(*sem):
    return pltpu.CompilerParams(dimension_semantics=sem, vmem_limit_bytes=VMEM_LIMIT_BYTES)


def _group(i, tm, tc, ll):
    row0 = i * tm
    return jnp.where(row0 < tc, 0, 1 + (row0 - tc) // ll)


def _norm(r):
    mu = jnp.mean(r, axis=-1, keepdims=True)
    d = r - mu
    var = jnp.mean(d * d, axis=-1, keepdims=True)
    return d * lax.rsqrt(var + LN_EPS)


def _silu(x):
    return x * jax.nn.sigmoid(x)


def _mod_kernel(c_ref, w_ref, b_ref, o_ref):
    s = _silu(c_ref[...]).astype(BF16)
    o_ref[0] = jnp.dot(s, w_ref[0].astype(BF16), preferred_element_type=F32) + b_ref[0]


def _modulation(cond, ada_w, ada_b):
    L, D, N = ada_w.shape
    tn = _tile(N, 1024)
    return pl.pallas_call(
        _mod_kernel,
        grid=(L, N // tn),
        in_specs=[pl.BlockSpec((MOD_ROWS, D), lambda l, j: (0, 0)),
                  pl.BlockSpec((1, D, tn), lambda l, j: (l, 0, j)),
                  pl.BlockSpec((1, 1, tn), lambda l, j: (l, 0, j))],
        out_specs=pl.BlockSpec((1, MOD_ROWS, tn), lambda l, j: (l, 0, j)),
        out_shape=jax.ShapeDtypeStruct((L, MOD_ROWS, N), F32),
        compiler_params=_params("parallel", "parallel"),
        name="adaln_modulation",
    )(cond, ada_w, ada_b.reshape(L, 1, N))


def _mm_mod_kernel(*refs, tm, tc, ll, has_bias, gelu):
    if has_bias:
        x_ref, sh_ref, sc_ref, w_ref, b_ref, o_ref, hb_ref = refs
    else:
        x_ref, sh_ref, sc_ref, w_ref, o_ref, hb_ref = refs
    i, j = pl.program_id(0), pl.program_id(1)

    @pl.when(j == 0)
    def _():
        g = _group(i, tm, tc, ll)
        h = x_ref[...] * (1.0 + sc_ref[pl.ds(g, 1), :]) + sh_ref[pl.ds(g, 1), :]
        hb_ref[...] = h.astype(BF16)

    acc = jnp.dot(hb_ref[...], w_ref[...], preferred_element_type=F32)
    if has_bias:
        acc = acc + b_ref[...]
    if gelu:
        acc = 0.5 * acc * (1.0 + lax.erf(acc * (2.0 ** -0.5)))
    o_ref[...] = acc.astype(o_ref.dtype)


def _mm_mod(x, mod, shift_chunk, w, bias, *, tc, ll, gelu=False):
    T, D = x.shape
    N = w.shape[1]
    tm = _row_tile(tc, ll, 1024)
    tn = _tile(N, 1024)
    in_specs = [pl.BlockSpec((tm, D), lambda i, j: (i, 0)),
                pl.BlockSpec((MOD_ROWS, D), lambda i, j: (0, shift_chunk)),
                pl.BlockSpec((MOD_ROWS, D), lambda i, j: (0, shift_chunk + 1)),
                pl.BlockSpec((D, tn), lambda i, j: (0, j))]
    args = [x, mod, mod, w]
    if bias is not None:
        in_specs.append(pl.BlockSpec((1, tn), lambda i, j: (0, j)))
        args.append(bias.reshape(1, N))
    return pl.pallas_call(
        functools.partial(_mm_mod_kernel, tm=tm, tc=tc, ll=ll, has_bias=bias is not None, gelu=gelu),
        grid=(T // tm, N // tn),
        in_specs=in_specs,
        out_specs=pl.BlockSpec((tm, tn), lambda i, j: (i, j)),
        out_shape=jax.ShapeDtypeStruct((T, N), BF16),
        scratch_shapes=[pltpu.VMEM((tm, D), BF16)],
        compiler_params=_params("parallel", "arbitrary"),
        name="modulate_in_proj",
    )(*args)


def _out_ln_kernel(*refs, n_y, nk, tm, tc, ll, alpha):
    y_refs = refs[:n_y]
    w_ref, x_ref, gate_ref, lng_ref, lnb_ref, o_ref, acc_ref = refs[n_y:]
    i, k = pl.program_id(0), pl.program_id(1)

    @pl.when(k == 0)
    def _():
        acc_ref[...] = jnp.zeros_like(acc_ref)

    if n_y == 1:
        y = y_refs[0][...]
    else:
        y = (y_refs[0][...].astype(F32) + y_refs[1][...].astype(F32)).astype(BF16)
    acc_ref[...] += jnp.dot(y, w_ref[...], preferred_element_type=F32)

    @pl.when(k == nk - 1)
    def _():
        g = _group(i, tm, tc, ll)
        r = alpha * x_ref[...] + gate_ref[pl.ds(g, 1), :] * acc_ref[...]
        o_ref[...] = _norm(r) * lng_ref[...] + lnb_ref[...]


def _out_ln(ys, w, x, mod, gate_chunk, ln_g, ln_b, *, tc, ll, alpha):
    T, D = x.shape
    K = w.shape[0]
    tm = _row_tile(tc, ll, 512)
    tk = _tile(K, 2048)
    nk = K // tk
    in_specs = [pl.BlockSpec((tm, tk), lambda i, k: (i, k)) for _ in ys]
    in_specs += [pl.BlockSpec((tk, D), lambda i, k: (k, 0)),
                 pl.BlockSpec((tm, D), lambda i, k: (i, 0)),
                 pl.BlockSpec((MOD_ROWS, D), lambda i, k: (0, gate_chunk)),
                 pl.BlockSpec((1, D), lambda i, k: (0, 0)),
                 pl.BlockSpec((1, D), lambda i, k: (0, 0))]
    return pl.pallas_call(
        functools.partial(_out_ln_kernel, n_y=len(ys), nk=nk, tm=tm, tc=tc, ll=ll, alpha=alpha),
        grid=(T // tm, nk),
        in_specs=in_specs,
        out_specs=pl.BlockSpec((tm, D), lambda i, k: (i, 0)),
        out_shape=jax.ShapeDtypeStruct((T, D), F32),
        scratch_shapes=[pltpu.VMEM((tm, D), F32)],
        compiler_params=_params("parallel", "arbitrary"),
        name="out_proj_residual_ln",
    )(*ys, w, x, mod, ln_g.reshape(1, D), ln_b.reshape(1, D))


def _ret_item(n, lay):
    ncx, nc_c, nc_l = lay
    is_ctx = n < ncx
    m = jnp.where(is_ctx, n, n - ncx)
    seq = jnp.where(is_ctx, m // nc_c, m // nc_l)
    c = jnp.where(is_ctx, m % nc_c, m % nc_l)
    nc = jnp.where(is_ctx, nc_c, nc_l)
    return is_ctx, seq, c, nc, n - c


def _ret_kernel(*refs, n_prev, lay, scale):
    dec_ref, qf, kf, vf, gf, qb, kb, vb, gb, s0_ref = refs[:10]
    prev_refs = refs[10:10 + n_prev]
    yf_ref, yb_ref, out_ref, st_ref = refs[10 + n_prev:]
    hb, dk, dv = st_ref.shape[1:]
    h0 = pl.program_id(0) * hb
    is_ctx, _, c, nc, _ = _ret_item(pl.program_id(1), lay)
    C = RET_CHUNK

    @pl.when(c == 0)
    def _():
        st_ref[...] = jnp.where(is_ctx, 0.0, s0_ref[...])

    ri = lax.broadcasted_iota(I32, (C, C), 0)
    ci = lax.broadcasted_iota(I32, (C, C), 1)
    row = lax.broadcasted_iota(I32, (C, 1), 0).astype(F32)
    dirs = ((qf, kf, vf, gf, yf_ref), (qb, kb, vb, gb, yb_ref))
    for d, (q_ref, k_ref, v_ref, g_ref, y_ref) in enumerate(dirs):
        rel = ri - ci if d == 0 else ci - ri
        pos = row if d == 0 else (C - 1.0) - row
        for hh in range(hb):
            kc, vc = slice(hh * dk, (hh + 1) * dk), slice(hh * dv, (hh + 1) * dv)
            lg = -jnp.exp(jnp.full((1, 1), dec_ref[d, h0 + hh], F32))
            dmask = jnp.where(rel >= 0, jnp.exp(lg * jnp.maximum(rel, 0).astype(F32)), 0.0) * scale
            xi = jnp.exp(lg * (pos + 1.0))
            zeta = jnp.exp(lg * ((C - 1.0) - pos)) * scale
            q, k, v = q_ref[:, kc], k_ref[:, kc], v_ref[:, vc]
            s = st_ref[d, hh]
            scores = lax.dot_general(q, k, NT_DIMS, preferred_element_type=F32) * dmask
            o = (jnp.dot(scores.astype(BF16), v, preferred_element_type=F32)
                 + jnp.dot(q, s.astype(BF16), preferred_element_type=F32) * xi)
            kz = (k.astype(F32) * zeta).astype(BF16)
            st_ref[d, hh] = (jnp.exp(lg * float(C)) * s
                             + lax.dot_general(kz, v, TN_DIMS, preferred_element_type=F32))
            y_ref[:, vc] = (_silu(g_ref[:, vc].astype(F32)) * _norm(o)).astype(y_ref.dtype)

    @pl.when(jnp.logical_and(is_ctx, c == nc - 1))
    def _():
        for p, prev_ref in enumerate(prev_refs):
            out_ref[p] = prev_ref[...]
        out_ref[n_prev] = st_ref[...]


def _retention(a, decay, s0, s0_layer, prev_states, *, n_ctx_seq, ctx_len, lat_len):
    T = a.shape[0]
    H = RET_HEADS
    DK, DV = s0.shape[-2], s0.shape[-1]
    C = RET_CHUNK
    lay = (n_ctx_seq * ctx_len // C, ctx_len // C, lat_len // C)
    n_prev = 0 if prev_states is None else prev_states.shape[1]

    def fwd(n):
        return n

    def bwd(n):
        _, _, c, nc, first = _ret_item(n, lay)
        return first + (nc - 1 - c)

    def lat_seq(n):
        is_ctx, seq, _, _, _ = _ret_item(n, lay)
        return jnp.where(is_ctx, 0, seq)

    def ctx_seq(n):
        is_ctx, seq, _, _, _ = _ret_item(n, lay)
        return jnp.where(is_ctx, seq, n_ctx_seq - 1)

    hb = RET_HEADS_PER_STEP
    nh = H // hb

    def col_specs(rowf):
        return [pl.BlockSpec((C, hb * DK), lambda h, n: (rowf(n), h)),
                pl.BlockSpec((C, hb * DK), lambda h, n: (rowf(n), nh + h)),
                pl.BlockSpec((C, hb * DV), lambda h, n: (rowf(n), nh + h))]

    def state_spec(layers, seqf):
        return pl.BlockSpec((None, layers, 2, hb, DK, DV), lambda h, n: (seqf(n), 0, 0, h, 0, 0))

    in_specs = [pl.BlockSpec(memory_space=pltpu.SMEM)]
    in_specs += col_specs(fwd) + [pl.BlockSpec((C, hb * DV), lambda h, n: (fwd(n), 2 * nh + h))]
    in_specs += col_specs(bwd) + [pl.BlockSpec((C, hb * DV), lambda h, n: (bwd(n), 3 * nh + h))]
    in_specs.append(pl.BlockSpec((None, None, 2, hb, DK, DV), lambda h, n: (lat_seq(n), s0_layer, 0, h, 0, 0)))
    args = [decay] + [a] * 8 + [s0]
    if n_prev:
        in_specs.append(state_spec(n_prev, ctx_seq))
        args.append(prev_states)

    def kern(*refs):
        if n_prev:
            refs = refs[:10] + tuple(refs[10].at[p] for p in range(n_prev)) + refs[11:]
        _ret_kernel(*refs, n_prev=n_prev, lay=lay, scale=float(DK) ** -0.5)

    V = H * DV
    return pl.pallas_call(
        kern,
        grid=(nh, T // C),
        in_specs=in_specs,
        out_specs=[pl.BlockSpec((C, hb * DV), lambda h, n: (fwd(n), h)),
                   pl.BlockSpec((C, hb * DV), lambda h, n: (bwd(n), h)),
                   state_spec(n_prev + 1, ctx_seq)],
        out_shape=[jax.ShapeDtypeStruct((T, V), BF16), jax.ShapeDtypeStruct((T, V), BF16),
                   jax.ShapeDtypeStruct((n_ctx_seq, n_prev + 1, 2, H, DK, DV), F32)],
        scratch_shapes=[pltpu.VMEM((2, hb, DK, DV), F32)],
        compiler_params=_params("parallel", "arbitrary"),
        name="retention_core",
    )(*args)


def _gmlp_kernel(u_ref, v_ref, lng_ref, lnb_ref, ws_ref, bs_ref, o_ref):
    vn = (_norm(v_ref[...].astype(F32)) * lng_ref[...] + lnb_ref[...]).astype(BF16)
    gw = vn.shape[1] // GM_GROUPS
    for g in range(GM_GROUPS):
        cols = slice(g * gw, (g + 1) * gw)
        mixed = jnp.dot(ws_ref[g], vn[:, cols], preferred_element_type=F32) + bs_ref[:, g:g + 1]
        o_ref[:, cols] = (u_ref[:, cols].astype(F32) * mixed).astype(o_ref.dtype)


def _gmlp_core(z, ln_g, ln_b, w_s, b_s):
    T, W2 = z.shape
    W = W2 // 2
    C = GM_CHUNK
    return pl.pallas_call(
        _gmlp_kernel,
        grid=(T // C,),
        in_specs=[pl.BlockSpec((C, W), lambda n: (n, 0)),
                  pl.BlockSpec((C, W), lambda n: (n, 1)),
                  pl.BlockSpec((1, W), lambda n: (0, 0)),
                  pl.BlockSpec((1, W), lambda n: (0, 0)),
                  pl.BlockSpec((GM_GROUPS, C, C), lambda n: (0, 0, 0)),
                  pl.BlockSpec((C, GM_GROUPS), lambda n: (0, 0))],
        out_specs=pl.BlockSpec((C, W), lambda n: (n, 0)),
        out_shape=jax.ShapeDtypeStruct((T, W), BF16),
        compiler_params=_params("parallel"),
        name="gmlp_spatial_gate",
    )(z, z, ln_g.reshape(1, W), ln_b.reshape(1, W), w_s.astype(BF16), b_s.T)


def _sconv_kernel(b_ref, c_ref, x_ref, w_ref, o_ref, *, tm, tc, p_ctx, p_lat):
    i = pl.program_id(0)
    t = c_ref[...].astype(F32) * x_ref[...].astype(F32)
    period = jnp.where(i * tm < tc, p_ctx, p_lat)
    pos = lax.rem(lax.broadcasted_iota(I32, (tm, 1), 0), period)
    t_prev = jnp.where(pos == 0, 0.0, pltpu.roll(t, 1, 0))
    t_next = jnp.where(pos == period - 1, 0.0, pltpu.roll(t, tm - 1, 0))
    conv = t_prev * w_ref[0:1, :] + t * w_ref[1:2, :] + t_next * w_ref[2:3, :]
    o_ref[...] = (b_ref[...].astype(F32) * conv).astype(o_ref.dtype)


def _sconv_core(a, conv_w, *, tc, ll, p_ctx, p_lat):
    T, D3 = a.shape
    D = D3 // 3
    tm = _row_tile(tc, ll, 512)
    assert tm % p_ctx == 0 and tm % p_lat == 0
    tn = _tile(D, 512)
    nj = D // tn
    return pl.pallas_call(
        functools.partial(_sconv_kernel, tm=tm, tc=tc, p_ctx=p_ctx, p_lat=p_lat),
        grid=(T // tm, nj),
        in_specs=[pl.BlockSpec((tm, tn), lambda i, j: (i, j)),
                  pl.BlockSpec((tm, tn), lambda i, j: (i, nj + j)),
                  pl.BlockSpec((tm, tn), lambda i, j: (i, 2 * nj + j)),
                  pl.BlockSpec((3, tn), lambda i, j: (0, j))],
        out_specs=pl.BlockSpec((tm, tn), lambda i, j: (i, j)),
        out_shape=jax.ShapeDtypeStruct((T, D), BF16),
        compiler_params=_params("parallel", "parallel"),
        name="short_conv_gate",
    )(a, a, a, conv_w)


def _router_kernel(x_ref, sh_ref, sc_ref, wt_ref, b_ref, h_ref, idx_ref, gate_ref, rank_ref, cnt_ref,
                   carry_ref, *, tm, tc, ll):
    i = pl.program_id(0)
    E = N_EXPERTS

    @pl.when(i == 0)
    def _():
        carry_ref[...] = jnp.zeros_like(carry_ref)

    g = _group(i, tm, tc, ll)
    h = x_ref[...] * (1.0 + sc_ref[pl.ds(g, 1), :]) + sh_ref[pl.ds(g, 1), :]
    h_ref[...] = h
    hh = h.astype(BF16)
    hl = (h - hh.astype(F32)).astype(BF16)
    w = wt_ref[...]
    wh = w.astype(BF16)
    wl = (w - wh.astype(F32)).astype(BF16)
    logits = (lax.dot_general(wh, hh, NT_DIMS, preferred_element_type=F32)
              + lax.dot_general(wh, hl, NT_DIMS, preferred_element_type=F32)
              + lax.dot_general(wl, hh, NT_DIMS, preferred_element_type=F32)) + b_ref[...]
    e_iota = lax.broadcasted_iota(I32, (E, tm), 0)
    earlier = (lax.broadcasted_iota(I32, (tm, tm), 0) < lax.broadcasted_iota(I32, (tm, tm), 1)).astype(BF16)
    base = carry_ref[...]
    vals = logits
    tops = []
    for k in range(TOP_K):
        m = jnp.max(vals, axis=0, keepdims=True)
        idx = jnp.min(jnp.where(vals == m, e_iota, E), axis=0, keepdims=True)
        hit = e_iota == idx
        onehot = jnp.where(hit, 1.0, 0.0)
        before = jnp.dot(onehot.astype(BF16), earlier, preferred_element_type=F32)
        rank = jnp.sum(onehot * (base + before), axis=0, keepdims=True)
        base = base + jnp.sum(onehot, axis=1, keepdims=True)
        idx_ref[k:k + 1, :] = idx
        rank_ref[k:k + 1, :] = rank.astype(I32)
        tops.append(m)
        vals = jnp.where(hit, -jnp.inf, vals)
    carry_ref[...] = base
    cnt_ref[...] = base
    exps = [jnp.exp(t - tops[0]) for t in tops]
    den = exps[0]
    for e in exps[1:]:
        den = den + e
    for k in range(TOP_K):
        gate_ref[k:k + 1, :] = exps[k] / den


def _router(x, mod, shift_chunk, router_w, router_b, *, tc, ll):
    T, D = x.shape
    E = N_EXPERTS
    tm = _row_tile(tc, ll, 256)
    row = lambda i: (0, i)
    return pl.pallas_call(
        functools.partial(_router_kernel, tm=tm, tc=tc, ll=ll),
        grid=(T // tm,),
        in_specs=[pl.BlockSpec((tm, D), lambda i: (i, 0)),
                  pl.BlockSpec((MOD_ROWS, D), lambda i: (0, shift_chunk)),
                  pl.BlockSpec((MOD_ROWS, D), lambda i: (0, shift_chunk + 1)),
                  pl.BlockSpec((E, D), lambda i: (0, 0)),
                  pl.BlockSpec((E, 1), lambda i: (0, 0))],
        out_specs=[pl.BlockSpec((tm, D), lambda i: (i, 0)),
                   pl.BlockSpec((TOP_K, tm), row),
                   pl.BlockSpec((TOP_K, tm), row),
                   pl.BlockSpec((TOP_K, tm), row),
                   pl.BlockSpec((E, 1), lambda i: (0, 0))],
        out_shape=[jax.ShapeDtypeStruct((T, D), F32),
                   jax.ShapeDtypeStruct((TOP_K, T), I32),
                   jax.ShapeDtypeStruct((TOP_K, T), F32),
                   jax.ShapeDtypeStruct((TOP_K, T), I32),
                   jax.ShapeDtypeStruct((E, 1), F32)],
        scratch_shapes=[pltpu.VMEM((E, 1), F32)],
        compiler_params=_params("arbitrary"),
        name="moe_router",
    )(x, mod, mod, router_w.T, router_b.reshape(E, 1))


def _dispatch_kernel(lo_ref, hi_ref, dest_ref, h_ref, xs_ref, zero_ref, sem, *, n_pad):
    tm = h_ref.shape[0]

    @pl.when(pl.program_id(0) == 0)
    def _():
        zero_ref[...] = jnp.zeros_like(zero_ref)

        def zero_row(r, carry):
            pltpu.make_async_copy(zero_ref.at[pl.ds(0, 1)], xs_ref.at[pl.ds(r, 1)], sem).start()
            return carry

        def zero_expert(e, carry):
            return lax.fori_loop(lo_ref[e], hi_ref[e], zero_row, carry)

        lax.fori_loop(0, N_EXPERTS, zero_expert, 0)
        for _ in range(n_pad // tm):
            pltpu.make_async_copy(h_ref, xs_ref.at[pl.ds(0, tm)], sem).wait()

    def row_copy(r, k):
        return pltpu.make_async_copy(h_ref.at[pl.ds(r, 1)], xs_ref.at[pl.ds(dest_ref[k, r], 1)], sem)

    def issue(r, carry):
        for k in range(TOP_K):
            row_copy(r, k).start()
        return carry

    lax.fori_loop(0, tm, issue, 0)
    for k in range(TOP_K):
        pltpu.make_async_copy(h_ref, xs_ref.at[pl.ds(0, tm)], sem).wait()


def _dispatch(h, dest, pad_lo, pad_hi, n_rows):
    T, D = h.shape
    tm = _tile(T, 256)
    n_pad = n_rows - T * TOP_K
    assert n_pad % tm == 0
    return pl.pallas_call(
        functools.partial(_dispatch_kernel, n_pad=n_pad),
        grid_spec=pltpu.PrefetchScalarGridSpec(
            num_scalar_prefetch=2,
            grid=(T // tm,),
            in_specs=[pl.BlockSpec((TOP_K, tm), lambda i, lo, hi: (0, i), memory_space=pltpu.SMEM),
                      pl.BlockSpec((tm, D), lambda i, lo, hi: (i, 0))],
            out_specs=pl.BlockSpec(memory_space=pl.ANY),
            scratch_shapes=[pltpu.VMEM((8, D), h.dtype), pltpu.SemaphoreType.DMA(())]),
        out_shape=jax.ShapeDtypeStruct((n_rows, D), h.dtype),
        compiler_params=_params("arbitrary"),
        name="moe_dispatch",
    )(pad_lo, pad_hi, dest, h)


def _expert_changed(be_ref, i):
    return jnp.logical_or(i == 0, be_ref[i] != be_ref[jnp.maximum(i - 1, 0)])


def _gate_up_kernel(be_ref, nu_ref, x_ref, wg_ref, wu_ref, bg_ref, bu_ref, o_ref, wgb_ref, wub_ref):
    i = pl.program_id(1)

    @pl.when(_expert_changed(be_ref, i))
    def _():
        wgb_ref[...] = wg_ref[...].astype(BF16)
        wub_ref[...] = wu_ref[...].astype(BF16)

    @pl.when(i < nu_ref[0])
    def _():
        x = x_ref[...].astype(BF16)
        gate = jnp.dot(x, wgb_ref[...], preferred_element_type=F32) + bg_ref[...]
        up = jnp.dot(x, wub_ref[...], preferred_element_type=F32) + bu_ref[...]
        gate = jnp.minimum(gate, SWIGLU_LIMIT)
        up = jnp.clip(up, -SWIGLU_LIMIT, SWIGLU_LIMIT)
        act = (up + 1.0) * (gate * jax.nn.sigmoid(SWIGLU_ALPHA * gate))
        o_ref[...] = act.astype(o_ref.dtype)

    @pl.when(i >= nu_ref[0])
    def _():
        o_ref[...] = jnp.zeros_like(o_ref)


def _gate_up(xs, blk_expert, n_used, w_gu, b_gu, layer):
    R, D = xs.shape
    F2 = w_gu.shape[-1]
    F = F2 // 2
    tm = MOE_ROWS
    tf = _tile(F, 1024)
    nf = F // tf
    nb = R // tm
    last = lambda i, nu: jnp.minimum(i, nu[0] - 1)
    return pl.pallas_call(
        _gate_up_kernel,
        grid_spec=pltpu.PrefetchScalarGridSpec(
            num_scalar_prefetch=2,
            grid=(nf, nb),
            in_specs=[pl.BlockSpec((tm, D), lambda j, i, be, nu: (last(i, nu), 0)),
                      pl.BlockSpec((None, None, D, tf), lambda j, i, be, nu: (layer, be[i], 0, j)),
                      pl.BlockSpec((None, None, D, tf), lambda j, i, be, nu: (layer, be[i], 0, nf + j)),
                      pl.BlockSpec((None, None, 1, tf), lambda j, i, be, nu: (layer, be[i], 0, j)),
                      pl.BlockSpec((None, None, 1, tf), lambda j, i, be, nu: (layer, be[i], 0, nf + j))],
            out_specs=pl.BlockSpec((tm, tf), lambda j, i, be, nu: (i, j)),
            scratch_shapes=[pltpu.VMEM((D, tf), BF16), pltpu.VMEM((D, tf), BF16)]),
        out_shape=jax.ShapeDtypeStruct((R, F), BF16),
        compiler_params=_params("arbitrary", "arbitrary"),
        name="moe_gate_up",
    )(blk_expert, n_used, xs, w_gu, w_gu, b_gu, b_gu)


def _down_kernel(be_ref, nu_ref, a_ref, w_ref, b_ref, o_ref, wb_ref):
    i = pl.program_id(1)

    @pl.when(_expert_changed(be_ref, i))
    def _():
        wb_ref[...] = w_ref[...].astype(BF16)

    @pl.when(i < nu_ref[0])
    def _():
        o_ref[...] = jnp.dot(a_ref[...], wb_ref[...], preferred_element_type=F32) + b_ref[...]

    @pl.when(i >= nu_ref[0])
    def _():
        o_ref[...] = jnp.zeros_like(o_ref)


def _down(act, blk_expert, n_used, w_down, b_down, layer):
    R, F = act.shape
    D = w_down.shape[-1]
    tm = MOE_ROWS
    tn = _tile(D, 2048)
    nb = R // tm
    last = lambda i, nu: jnp.minimum(i, nu[0] - 1)
    return pl.pallas_call(
        _down_kernel,
        grid_spec=pltpu.PrefetchScalarGridSpec(
            num_scalar_prefetch=2,
            grid=(D // tn, nb),
            in_specs=[pl.BlockSpec((tm, F), lambda j, i, be, nu: (last(i, nu), 0)),
                      pl.BlockSpec((None, None, F, tn), lambda j, i, be, nu: (layer, be[i], 0, j)),
                      pl.BlockSpec((None, None, 1, tn), lambda j, i, be, nu: (layer, be[i], 0, j))],
            out_specs=pl.BlockSpec((tm, tn), lambda j, i, be, nu: (i, j)),
            scratch_shapes=[pltpu.VMEM((F, tn), BF16)]),
        out_shape=jax.ShapeDtypeStruct((R, D), F32),
        compiler_params=_params("arbitrary", "arbitrary"),
        name="moe_down",
    )(blk_expert, n_used, act, w_down, b_down)


def _combine_kernel(dest_ref, gt_ref, x_ref, gate_ref, lng_ref, lnb_ref, ys_ref, *rest, tm, tc, ll, alpha):
    *o_refs, buf_ref, sem = rest
    i = pl.program_id(0)

    def row_copy(r, k):
        return pltpu.make_async_copy(ys_ref.at[pl.ds(dest_ref[k, r], 1)], buf_ref.at[k, pl.ds(r, 1)], sem)

    def issue(r, carry):
        for k in range(TOP_K):
            row_copy(r, k).start()
        return carry

    lax.fori_loop(0, tm, issue, 0)
    for k in range(TOP_K):
        pltpu.make_async_copy(ys_ref.at[pl.ds(0, tm)], buf_ref.at[k], sem).wait()
    f = gt_ref[:, 0:1] * buf_ref[0]
    for k in range(1, TOP_K):
        f = f + gt_ref[:, k:k + 1] * buf_ref[k]
    g = _group(i, tm, tc, ll)
    r = alpha * x_ref[...] + gate_ref[pl.ds(g, 1), :] * f
    out = _norm(r) * lng_ref[...] + lnb_ref[...]
    if len(o_refs) == 1:
        o_refs[0][...] = out
    else:
        @pl.when(i * tm < tc)
        def _():
            o_refs[0][...] = out

        @pl.when(i * tm >= tc)
        def _():
            o_refs[1][...] = out


def _combine(ys, dest, gates_t, x, mod, gate_chunk, ln_g, ln_b, *, tc, ll, alpha, split):
    T, D = x.shape
    tm = _row_tile(tc, ll, 256)
    nbc = tc // tm
    if split:
        out_specs = [pl.BlockSpec((tm, D), lambda i: (jnp.minimum(i, nbc - 1), 0)),
                     pl.BlockSpec((tm, D), lambda i: (jnp.maximum(i - nbc, 0), 0))]
        out_shape = [jax.ShapeDtypeStruct((tc, D), F32), jax.ShapeDtypeStruct((T - tc, D), F32)]
    else:
        out_specs = pl.BlockSpec((tm, D), lambda i: (i, 0))
        out_shape = jax.ShapeDtypeStruct((T, D), F32)
    return pl.pallas_call(
        functools.partial(_combine_kernel, tm=tm, tc=tc, ll=ll, alpha=alpha),
        grid=(T // tm,),
        in_specs=[pl.BlockSpec((TOP_K, tm), lambda i: (0, i), memory_space=pltpu.SMEM),
                  pl.BlockSpec((tm, TOP_K), lambda i: (i, 0)),
                  pl.BlockSpec((tm, D), lambda i: (i, 0)),
                  pl.BlockSpec((MOD_ROWS, D), lambda i: (0, gate_chunk)),
                  pl.BlockSpec((1, D), lambda i: (0, 0)),
                  pl.BlockSpec((1, D), lambda i: (0, 0)),
                  pl.BlockSpec(memory_space=pl.ANY)],
        out_specs=out_specs,
        out_shape=out_shape,
        scratch_shapes=[pltpu.VMEM((TOP_K, tm, D), F32), pltpu.SemaphoreType.DMA(())],
        compiler_params=_params("arbitrary"),
        name="moe_combine_residual_ln",
    )(dest, gates_t, x, mod, ln_g.reshape(1, D), ln_b.reshape(1, D), ys)


def _moe_block(x, mod, layer, router_w, router_b, w_gu, b_gu, w_down, b_down, ln_g, ln_b, *, tc, ll, alpha,
               split):
    T, D = x.shape
    E = N_EXPERTS
    h, idx, gates, rank, cnt = _router(x, mod, 3, router_w, router_b, tc=tc, ll=ll)
    counts = cnt[:, 0].astype(I32)
    padded = (counts + MOE_ROWS - 1) // MOE_ROWS * MOE_ROWS
    pad_ends = jnp.cumsum(padded)
    pad_starts = pad_ends - padded
    onehot = idx[:, :, None] == jnp.arange(E, dtype=I32)
    dest = jnp.sum(jnp.where(onehot, pad_starts, 0), axis=-1) + rank
    nb = pl.cdiv(T * TOP_K + E * (MOE_ROWS - 1), MOE_ROWS)
    n_rows = nb * MOE_ROWS
    blk_start = jnp.arange(nb, dtype=I32) * MOE_ROWS
    blk_expert = jnp.minimum(jnp.sum(blk_start[:, None] >= pad_ends[None, :], axis=1), E - 1).astype(I32)
    n_used = (pad_ends[-1:] // MOE_ROWS).astype(I32)
    pad_hi = pad_ends.at[E - 1].set(n_rows).astype(I32)
    xs = _dispatch(h, dest, (pad_starts + counts).astype(I32), pad_hi, n_rows)
    E4 = b_gu.shape
    act = _gate_up(xs, blk_expert, n_used, w_gu, b_gu.reshape(E4[0], E4[1], 1, E4[2]), layer)
    ys = _down(act, blk_expert, n_used, w_down, b_down.reshape(b_down.shape[0], E, 1, D), layer)
    return _combine(ys, dest, gates.T, x, mod, 5, ln_g, ln_b, tc=tc, ll=ll, alpha=alpha, split=split)


def kernel(x_prompt, x_sample, state_ret, c, c_ctx, ada_w, ada_b, ln_g, ln_b, ret_w_in, ret_w_out, ret_decay,
           gm_w_in, gm_b_in, gm_ln_g, gm_ln_b, gm_w_s, gm_b_s, gm_w_out, sc_w_in, sc_conv, sc_w_out,
           moe_router_w, moe_router_b, moe_w_gu, moe_b_gu, moe_w_down, moe_b_down):
    B, S, D = x_prompt.shape
    BL, SL, _ = x_sample.shape
    depth = ada_w.shape[0]
    assert 1 + BL <= MOD_ROWS
    assert S % RET_CHUNK == 0 and SL % RET_CHUNK == 0 and SL % GRID_W == 0
    tc, ll = B * S, SL
    alpha = (2.0 * depth) ** 0.25

    x = jnp.concatenate([x_prompt.reshape(tc, D), x_sample.reshape(BL * SL, D)], axis=0)
    cond = jnp.concatenate([c_ctx[None], c, jnp.zeros((MOD_ROWS - 1 - BL, D), F32)], axis=0)
    mods = _modulation(cond, ada_w, ada_b)
    states = None
    kw = dict(tc=tc, ll=ll)
    for i in range(depth):
        kind, j = i % N_MIXERS, i // N_MIXERS
        mod = mods[i]
        if kind == 0:
            a = _mm_mod(x, mod, 0, ret_w_in[j].astype(BF16), None, **kw)
            yf, yb, states = _retention(a, ret_decay[j], state_ret, j, states, n_ctx_seq=B, ctx_len=S, lat_len=SL)
            ys, w_out = [yf, yb], ret_w_out[j]
        elif kind == 1:
            z = _mm_mod(x, mod, 0, gm_w_in[j].astype(BF16), gm_b_in[j], gelu=True, **kw)
            ys, w_out = [_gmlp_core(z, gm_ln_g[j], gm_ln_b[j], gm_w_s[j], gm_b_s[j])], gm_w_out[j]
        else:
            a = _mm_mod(x, mod, 0, sc_w_in[j].astype(BF16), None, **kw)
            ys, w_out = [_sconv_core(a, sc_conv[j], p_ctx=S, p_lat=GRID_W, **kw)], sc_w_out[j]
        x = _out_ln(ys, w_out.astype(BF16), x, mod, 2, ln_g[i, 0], ln_b[i, 0], alpha=alpha, **kw)
        x = _moe_block(x, mod, i, moe_router_w[i], moe_router_b[i], moe_w_gu, moe_b_gu, moe_w_down, moe_b_down,
                       ln_g[i, 1], ln_b[i, 1], alpha=alpha, split=i == depth - 1, **kw)
    x_ctx, x_lat = x
    return (x_ctx.reshape(B, S, D), x_lat.reshape(BL, SL, D), states)
```

```python
import functools
import math

import jax
import jax.numpy as jnp
from jax import lax
from jax.experimental import pallas as pl
from jax.experimental.pallas import tpu as pltpu

F32 = jnp.float32
BF16 = jnp.bfloat16
I32 = jnp.int32
U32 = jnp.uint32

N_MIXERS = 3
RET_HEADS = 8
RET_CHUNK = 128
GM_GROUPS = 8
GM_CHUNK = 128
GRID_W = 64
N_EXPERTS = 32
TOP_K = 4
SWIGLU_LIMIT = 7.0
SWIGLU_ALPHA = 1.702
LN_EPS = 1e-5

MOD_ROWS = 8
VMEM_LIMIT_BYTES = 52 * 1024 * 1024
MOE_ROWS = 256
RET_HEADS_PER_STEP = 2

NT_DIMS = (((1,), (1,)), ((), ()))
TN_DIMS = (((0,), (0,)), ((), ()))


def _tile(n, pref):
    t = min(n, pref)
    while n % t:
        t //= 2
    return t


def _row_tile(tc, ll, pref):
    return _tile(math.gcd(tc, ll), pref)


def _params(*sem):
    return pltpu.CompilerParams(dimension_semantics=sem, vmem_limit_bytes=VMEM_LIMIT_BYTES)


def _group(i, tm, tc, ll):
    row0 = i * tm
    return jnp.where(row0 < tc, 0, 1 + (row0 - tc) // ll)


def _norm(r):
    mu = jnp.mean(r, axis=-1, keepdims=True)
    d = r - mu
    var = jnp.mean(d * d, axis=-1, keepdims=True)
    return d * lax.rsqrt(var + LN_EPS)


def _silu(x):
    return x * jax.nn.sigmoid(x)


def _pack_halves(x):
    n2 = x.shape[1] // 2
    lo = pltpu.bitcast(x[:, :n2].astype(BF16).astype(F32), U32)
    hi = pltpu.bitcast(x[:, n2:].astype(BF16).astype(F32), U32)
    return (lo >> 16) | (hi & jnp.uint32(0xFFFF0000))


def _unpack_halves(p):
    return pltpu.bitcast(p << 16, F32), pltpu.bitcast(p & jnp.uint32(0xFFFF0000), F32)


def _mod_kernel(c_ref, w_ref, b_ref, o_ref):
    s = _silu(c_ref[...]).astype(BF16)
    o_ref[0] = jnp.dot(s, w_ref[0].astype(BF16), preferred_element_type=F32) + b_ref[0]


def _modulation(cond, ada_w, ada_b):
    L, D, N = ada_w.shape
    tn = _tile(N, 1024)
    return pl.pallas_call(
        _mod_kernel,
        grid=(L, N // tn),
        in_specs=[pl.BlockSpec((MOD_ROWS, D), lambda l, j: (0, 0)),
                  pl.BlockSpec((1, D, tn), lambda l, j: (l, 0, j)),
                  pl.BlockSpec((1, 1, tn), lambda l, j: (l, 0, j))],
        out_specs=pl.BlockSpec((1, MOD_ROWS, tn), lambda l, j: (l, 0, j)),
        out_shape=jax.ShapeDtypeStruct((L, MOD_ROWS, N), F32),
        compiler_params=_params("parallel", "parallel"),
        name="adaln_modulation",
    )(cond, ada_w, ada_b.reshape(L, 1, N))


def _mm_mod_kernel(*refs, tm, tc, ll, has_bias, gelu):
    if has_bias:
        x_ref, sh_ref, sc_ref, w_ref, b_ref, o_ref, hb_ref = refs
    else:
        x_ref, sh_ref, sc_ref, w_ref, o_ref, hb_ref = refs
    i, j = pl.program_id(0), pl.program_id(1)

    @pl.when(j == 0)
    def _():
        g = _group(i, tm, tc, ll)
        h = x_ref[...] * (1.0 + sc_ref[pl.ds(g, 1), :]) + sh_ref[pl.ds(g, 1), :]
        hb_ref[...] = h.astype(BF16)

    acc = jnp.dot(hb_ref[...], w_ref[...], preferred_element_type=F32)
    if has_bias:
        acc = acc + b_ref[...]
    if gelu:
        acc = 0.5 * acc * (1.0 + lax.erf(acc * (2.0 ** -0.5)))
    o_ref[...] = acc.astype(o_ref.dtype)


def _mm_mod(x, mod, shift_chunk, w, bias, *, tc, ll, gelu=False):
    T, D = x.shape
    N = w.shape[1]
    tm = _row_tile(tc, ll, 1024)
    tn = _tile(N, 1024)
    in_specs = [pl.BlockSpec((tm, D), lambda i, j: (i, 0)),
                pl.BlockSpec((MOD_ROWS, D), lambda i, j: (0, shift_chunk)),
                pl.BlockSpec((MOD_ROWS, D), lambda i, j: (0, shift_chunk + 1)),
                pl.BlockSpec((D, tn), lambda i, j: (0, j))]
    args = [x, mod, mod, w]
    if bias is not None:
        in_specs.append(pl.BlockSpec((1, tn), lambda i, j: (0, j)))
        args.append(bias.reshape(1, N))
    return pl.pallas_call(
        functools.partial(_mm_mod_kernel, tm=tm, tc=tc, ll=ll, has_bias=bias is not None, gelu=gelu),
        grid=(T // tm, N // tn),
        in_specs=in_specs,
        out_specs=pl.BlockSpec((tm, tn), lambda i, j: (i, j)),
        out_shape=jax.ShapeDtypeStruct((T, N), BF16),
        scratch_shapes=[pltpu.VMEM((tm, D), BF16)],
        compiler_params=_params("parallel", "arbitrary"),
        name="modulate_in_proj",
    )(*args)


def _out_ln_kernel(*refs, n_y, nk, tm, tc, ll, alpha):
    y_refs = refs[:n_y]
    w_ref, x_ref, gate_ref, lng_ref, lnb_ref, o_ref, acc_ref = refs[n_y:]
    i, k = pl.program_id(0), pl.program_id(1)

    @pl.when(k == 0)
    def _():
        acc_ref[...] = jnp.zeros_like(acc_ref)

    if n_y == 1:
        y = y_refs[0][...]
    else:
        y = (y_refs[0][...].astype(F32) + y_refs[1][...].astype(F32)).astype(BF16)
    acc_ref[...] += jnp.dot(y, w_ref[...], preferred_element_type=F32)

    @pl.when(k == nk - 1)
    def _():
        g = _group(i, tm, tc, ll)
        r = alpha * x_ref[...] + gate_ref[pl.ds(g, 1), :] * acc_ref[...]
        o_ref[...] = _norm(r) * lng_ref[...] + lnb_ref[...]


def _out_ln(ys, w, x, mod, gate_chunk, ln_g, ln_b, *, tc, ll, alpha):
    T, D = x.shape
    K = w.shape[0]
    tm = _row_tile(tc, ll, 512)
    tk = _tile(K, 2048)
    nk = K // tk
    in_specs = [pl.BlockSpec((tm, tk), lambda i, k: (i, k)) for _ in ys]
    in_specs += [pl.BlockSpec((tk, D), lambda i, k: (k, 0)),
                 pl.BlockSpec((tm, D), lambda i, k: (i, 0)),
                 pl.BlockSpec((MOD_ROWS, D), lambda i, k: (0, gate_chunk)),
                 pl.BlockSpec((1, D), lambda i, k: (0, 0)),
                 pl.BlockSpec((1, D), lambda i, k: (0, 0))]
    return pl.pallas_call(
        functools.partial(_out_ln_kernel, n_y=len(ys), nk=nk, tm=tm, tc=tc, ll=ll, alpha=alpha),
        grid=(T // tm, nk),
        in_specs=in_specs,
        out_specs=pl.BlockSpec((tm, D), lambda i, k: (i, 0)),
        out_shape=jax.ShapeDtypeStruct((T, D), F32),
        scratch_shapes=[pltpu.VMEM((tm, D), F32)],
        compiler_params=_params("parallel", "arbitrary"),
        name="out_proj_residual_ln",
    )(*ys, w, x, mod, ln_g.reshape(1, D), ln_b.reshape(1, D))


def _ret_item(n, lay):
    ncx, nc_c, nc_l = lay
    is_ctx = n < ncx
    m = jnp.where(is_ctx, n, n - ncx)
    seq = jnp.where(is_ctx, m // nc_c, m // nc_l)
    c = jnp.where(is_ctx, m % nc_c, m % nc_l)
    nc = jnp.where(is_ctx, nc_c, nc_l)
    return is_ctx, seq, c, nc, n - c


def _ret_kernel(*refs, n_prev, lay, scale):
    dec_ref, qf, kf, vf, gf, qb, kb, vb, gb, s0_ref = refs[:10]
    prev_refs = refs[10:10 + n_prev]
    yf_ref, yb_ref, out_ref, st_ref = refs[10 + n_prev:]
    hb, dk, dv = st_ref.shape[1:]
    h0 = pl.program_id(0) * hb
    is_ctx, _, c, nc, _ = _ret_item(pl.program_id(1), lay)
    C = RET_CHUNK

    @pl.when(c == 0)
    def _():
        st_ref[...] = jnp.where(is_ctx, 0.0, s0_ref[...])

    ri = lax.broadcasted_iota(I32, (C, C), 0)
    ci = lax.broadcasted_iota(I32, (C, C), 1)
    row = lax.broadcasted_iota(I32, (C, 1), 0).astype(F32)
    dirs = ((qf, kf, vf, gf, yf_ref), (qb, kb, vb, gb, yb_ref))
    for d, (q_ref, k_ref, v_ref, g_ref, y_ref) in enumerate(dirs):
        rel = ri - ci if d == 0 else ci - ri
        pos = row if d == 0 else (C - 1.0) - row
        for hh in range(hb):
            kc, vc = slice(hh * dk, (hh + 1) * dk), slice(hh * dv, (hh + 1) * dv)
            lg = -jnp.exp(jnp.full((1, 1), dec_ref[d, h0 + hh], F32))
            dmask = jnp.where(rel >= 0, jnp.exp(lg * jnp.maximum(rel, 0).astype(F32)), 0.0) * scale
            xi = jnp.exp(lg * (pos + 1.0))
            zeta = jnp.exp(lg * ((C - 1.0) - pos)) * scale
            q, k, v = q_ref[:, kc], k_ref[:, kc], v_ref[:, vc]
            s = st_ref[d, hh]
            scores = lax.dot_general(q, k, NT_DIMS, preferred_element_type=F32) * dmask
            o = (jnp.dot(scores.astype(BF16), v, preferred_element_type=F32)
                 + jnp.dot(q, s.astype(BF16), preferred_element_type=F32) * xi)
            kz = (k.astype(F32) * zeta).astype(BF16)
            st_ref[d, hh] = (jnp.exp(lg * float(C)) * s
                             + lax.dot_general(kz, v, TN_DIMS, preferred_element_type=F32))
            y_ref[:, vc] = (_silu(g_ref[:, vc].astype(F32)) * _norm(o)).astype(y_ref.dtype)

    @pl.when(jnp.logical_and(is_ctx, c == nc - 1))
    def _():
        for p, prev_ref in enumerate(prev_refs):
            out_ref[p] = prev_ref[...]
        out_ref[n_prev] = st_ref[...]


def _retention(a, decay, s0, s0_layer, prev_states, *, n_ctx_seq, ctx_len, lat_len):
    T = a.shape[0]
    H = RET_HEADS
    DK, DV = s0.shape[-2], s0.shape[-1]
    C = RET_CHUNK
    lay = (n_ctx_seq * ctx_len // C, ctx_len // C, lat_len // C)
    n_prev = 0 if prev_states is None else prev_states.shape[1]

    def fwd(n):
        return n

    def bwd(n):
        _, _, c, nc, first = _ret_item(n, lay)
        return first + (nc - 1 - c)

    def lat_seq(n):
        is_ctx, seq, _, _, _ = _ret_item(n, lay)
        return jnp.where(is_ctx, 0, seq)

    def ctx_seq(n):
        is_ctx, seq, _, _, _ = _ret_item(n, lay)
        return jnp.where(is_ctx, seq, n_ctx_seq - 1)

    hb = RET_HEADS_PER_STEP
    nh = H // hb

    def col_specs(rowf):
        return [pl.BlockSpec((C, hb * DK), lambda h, n: (rowf(n), h)),
                pl.BlockSpec((C, hb * DK), lambda h, n: (rowf(n), nh + h)),
                pl.BlockSpec((C, hb * DV), lambda h, n: (rowf(n), nh + h))]

    def state_spec(layers, seqf):
        return pl.BlockSpec((None, layers, 2, hb, DK, DV), lambda h, n: (seqf(n), 0, 0, h, 0, 0))

    in_specs = [pl.BlockSpec(memory_space=pltpu.SMEM)]
    in_specs += col_specs(fwd) + [pl.BlockSpec((C, hb * DV), lambda h, n: (fwd(n), 2 * nh + h))]
    in_specs += col_specs(bwd) + [pl.BlockSpec((C, hb * DV), lambda h, n: (bwd(n), 3 * nh + h))]
    in_specs.append(pl.BlockSpec((None, None, 2, hb, DK, DV), lambda h, n: (lat_seq(n), s0_layer, 0, h, 0, 0)))
    args = [decay] + [a] * 8 + [s0]
    if n_prev:
        in_specs.append(state_spec(n_prev, ctx_seq))
        args.append(prev_states)

    def kern(*refs):
        if n_prev:
            refs = refs[:10] + tuple(refs[10].at[p] for p in range(n_prev)) + refs[11:]
        _ret_kernel(*refs, n_prev=n_prev, lay=lay, scale=float(DK) ** -0.5)

    V = H * DV
    return pl.pallas_call(
        kern,
        grid=(nh, T // C),
        in_specs=in_specs,
        out_specs=[pl.BlockSpec((C, hb * DV), lambda h, n: (fwd(n), h)),
                   pl.BlockSpec((C, hb * DV), lambda h, n: (bwd(n), h)),
                   state_spec(n_prev + 1, ctx_seq)],
        out_shape=[jax.ShapeDtypeStruct((T, V), BF16), jax.ShapeDtypeStruct((T, V), BF16),
                   jax.ShapeDtypeStruct((n_ctx_seq, n_prev + 1, 2, H, DK, DV), F32)],
        scratch_shapes=[pltpu.VMEM((2, hb, DK, DV), F32)],
        compiler_params=_params("parallel", "arbitrary"),
        name="retention_core",
    )(*args)


def _gmlp_kernel(u_ref, v_ref, lng_ref, lnb_ref, ws_ref, bs_ref, o_ref):
    vn = (_norm(v_ref[...].astype(F32)) * lng_ref[...] + lnb_ref[...]).astype(BF16)
    gw = vn.shape[1] // GM_GROUPS
    for g in range(GM_GROUPS):
        cols = slice(g * gw, (g + 1) * gw)
        mixed = jnp.dot(ws_ref[g], vn[:, cols], preferred_element_type=F32) + bs_ref[:, g:g + 1]
        o_ref[:, cols] = (u_ref[:, cols].astype(F32) * mixed).astype(o_ref.dtype)


def _gmlp_core(z, ln_g, ln_b, w_s, b_s):
    T, W2 = z.shape
    W = W2 // 2
    C = GM_CHUNK
    return pl.pallas_call(
        _gmlp_kernel,
        grid=(T // C,),
        in_specs=[pl.BlockSpec((C, W), lambda n: (n, 0)),
                  pl.BlockSpec((C, W), lambda n: (n, 1)),
                  pl.BlockSpec((1, W), lambda n: (0, 0)),
                  pl.BlockSpec((1, W), lambda n: (0, 0)),
                  pl.BlockSpec((GM_GROUPS, C, C), lambda n: (0, 0, 0)),
                  pl.BlockSpec((C, GM_GROUPS), lambda n: (0, 0))],
        out_specs=pl.BlockSpec((C, W), lambda n: (n, 0)),
        out_shape=jax.ShapeDtypeStruct((T, W), BF16),
        compiler_params=_params("parallel"),
        name="gmlp_spatial_gate",
    )(z, z, ln_g.reshape(1, W), ln_b.reshape(1, W), w_s.astype(BF16), b_s.T)


def _sconv_kernel(b_ref, c_ref, x_ref, w_ref, o_ref, *, tm, tc, p_ctx, p_lat):
    i = pl.program_id(0)
    t = c_ref[...].astype(F32) * x_ref[...].astype(F32)
    period = jnp.where(i * tm < tc, p_ctx, p_lat)
    pos = lax.rem(lax.broadcasted_iota(I32, (tm, 1), 0), period)
    t_prev = jnp.where(pos == 0, 0.0, pltpu.roll(t, 1, 0))
    t_next = jnp.where(pos == period - 1, 0.0, pltpu.roll(t, tm - 1, 0))
    conv = t_prev * w_ref[0:1, :] + t * w_ref[1:2, :] + t_next * w_ref[2:3, :]
    o_ref[...] = (b_ref[...].astype(F32) * conv).astype(o_ref.dtype)


def _sconv_core(a, conv_w, *, tc, ll, p_ctx, p_lat):
    T, D3 = a.shape
    D = D3 // 3
    tm = _row_tile(tc, ll, 512)
    assert tm % p_ctx == 0 and tm % p_lat == 0
    tn = _tile(D, 512)
    nj = D // tn
    return pl.pallas_call(
        functools.partial(_sconv_kernel, tm=tm, tc=tc, p_ctx=p_ctx, p_lat=p_lat),
        grid=(T // tm, nj),
        in_specs=[pl.BlockSpec((tm, tn), lambda i, j: (i, j)),
                  pl.BlockSpec((tm, tn), lambda i, j: (i, nj + j)),
                  pl.BlockSpec((tm, tn), lambda i, j: (i, 2 * nj + j)),
                  pl.BlockSpec((3, tn), lambda i, j: (0, j))],
        out_specs=pl.BlockSpec((tm, tn), lambda i, j: (i, j)),
        out_shape=jax.ShapeDtypeStruct((T, D), BF16),
        compiler_params=_params("parallel", "parallel"),
        name="short_conv_gate",
    )(a, a, a, conv_w)


def _router_kernel(x_ref, sh_ref, sc_ref, wt_ref, b_ref, hp_ref, idx_ref, gate_ref, cnt_ref, carry_ref,
                   *, tm, tc, ll):
    i = pl.program_id(0)
    E = N_EXPERTS

    @pl.when(i == 0)
    def _():
        carry_ref[...] = jnp.zeros_like(carry_ref)

    g = _group(i, tm, tc, ll)
    h = x_ref[...] * (1.0 + sc_ref[pl.ds(g, 1), :]) + sh_ref[pl.ds(g, 1), :]
    hp_ref[...] = _pack_halves(h)
    hh = h.astype(BF16)
    hl = (h - hh.astype(F32)).astype(BF16)
    w = wt_ref[...]
    wh = w.astype(BF16)
    wl = (w - wh.astype(F32)).astype(BF16)
    logits = (lax.dot_general(wh, hh, NT_DIMS, preferred_element_type=F32)
              + lax.dot_general(wh, hl, NT_DIMS, preferred_element_type=F32)
              + lax.dot_general(wl, hh, NT_DIMS, preferred_element_type=F32)) + b_ref[...]
    e_iota = lax.broadcasted_iota(I32, (E, tm), 0)
    count = carry_ref[...]
    vals = logits
    tops = []
    for k in range(TOP_K):
        m = jnp.max(vals, axis=0, keepdims=True)
        idx = jnp.min(jnp.where(vals == m, e_iota, E), axis=0, keepdims=True)
        hit = e_iota == idx
        count = count + jnp.sum(jnp.where(hit, 1.0, 0.0), axis=1, keepdims=True)
        idx_ref[k:k + 1, :] = idx
        tops.append(m)
        vals = jnp.where(hit, -jnp.inf, vals)
    carry_ref[...] = count
    cnt_ref[...] = count
    exps = [jnp.exp(t - tops[0]) for t in tops]
    den = exps[0]
    for e in exps[1:]:
        den = den + e
    for k in range(TOP_K):
        gate_ref[k:k + 1, :] = exps[k] / den


def _router(x, mod, shift_chunk, router_w, router_b, *, tc, ll):
    T, D = x.shape
    E = N_EXPERTS
    tm = _row_tile(tc, ll, 256)
    row = lambda i: (0, i)
    return pl.pallas_call(
        functools.partial(_router_kernel, tm=tm, tc=tc, ll=ll),
        grid=(T // tm,),
        in_specs=[pl.BlockSpec((tm, D), lambda i: (i, 0)),
                  pl.BlockSpec((MOD_ROWS, D), lambda i: (0, shift_chunk)),
                  pl.BlockSpec((MOD_ROWS, D), lambda i: (0, shift_chunk + 1)),
                  pl.BlockSpec((E, D), lambda i: (0, 0)),
                  pl.BlockSpec((E, 1), lambda i: (0, 0))],
        out_specs=[pl.BlockSpec((tm, D // 2), lambda i: (i, 0)),
                   pl.BlockSpec((TOP_K, tm), row),
                   pl.BlockSpec((TOP_K, tm), row),
                   pl.BlockSpec((E, 1), lambda i: (0, 0))],
        out_shape=[jax.ShapeDtypeStruct((T, D // 2), U32),
                   jax.ShapeDtypeStruct((TOP_K, T), I32),
                   jax.ShapeDtypeStruct((TOP_K, T), F32),
                   jax.ShapeDtypeStruct((E, 1), F32)],
        scratch_shapes=[pltpu.VMEM((E, 1), F32)],
        compiler_params=_params("arbitrary"),
        name="moe_router",
    )(x, mod, mod, router_w.T, router_b.reshape(E, 1))


def _expert_changed(be_ref, i):
    return jnp.logical_or(i == 0, be_ref[i] != be_ref[jnp.maximum(i - 1, 0)])


def _gate_up_kernel(be_ref, nu_ref, tok0_ref, tokn_ref, hp_ref, wg_ref, wu_ref, bg_ref, bu_ref, o_ref,
                    wgb_ref, wub_ref, xa_ref, xb_ref, sems, *, nf):
    j, i = pl.program_id(0), pl.program_id(1)
    nu = nu_ref[0]
    tm = xa_ref.shape[0]

    def gather(tok_ref, x_ref, sem):
        for r in range(tm):
            pltpu.make_async_copy(hp_ref.at[pl.ds(tok_ref[0, r], 1)], x_ref.at[pl.ds(r, 1)], sem).start()

    def wait_rows(x_ref, sem):
        pltpu.make_async_copy(hp_ref.at[pl.ds(0, tm)], x_ref, sem).wait()

    @pl.when(_expert_changed(be_ref, i))
    def _():
        wgb_ref[...] = wg_ref[...].astype(BF16)
        wub_ref[...] = wu_ref[...].astype(BF16)

    @pl.when(jnp.logical_and(j == 0, i == 0))
    def _():
        gather(tok0_ref, xa_ref, sems.at[0])

    parity = lax.rem(j * nu + i, 2)

    def step(x_ref, sem, nx_ref, nsem):
        wait_rows(x_ref, sem)
        half = x_ref.shape[1]
        lo, hi = _unpack_halves(x_ref[...])
        lo, hi = lo.astype(BF16), hi.astype(BF16)
        gather(tokn_ref, nx_ref, nsem)

        def proj(w_ref, b_ref):
            return (jnp.dot(lo, w_ref[:half, :], preferred_element_type=F32)
                    + jnp.dot(hi, w_ref[half:, :], preferred_element_type=F32) + b_ref[...])

        gate = jnp.minimum(proj(wgb_ref, bg_ref), SWIGLU_LIMIT)
        up = jnp.clip(proj(wub_ref, bu_ref), -SWIGLU_LIMIT, SWIGLU_LIMIT)
        act = (up + 1.0) * (gate * jax.nn.sigmoid(SWIGLU_ALPHA * gate))
        o_ref[...] = act.astype(o_ref.dtype)

    used = i < nu

    @pl.when(jnp.logical_and(used, parity == 0))
    def _():
        step(xa_ref, sems.at[0], xb_ref, sems.at[1])

    @pl.when(jnp.logical_and(used, parity == 1))
    def _():
        step(xb_ref, sems.at[1], xa_ref, sems.at[0])

    @pl.when(jnp.logical_not(used))
    def _():
        o_ref[...] = jnp.zeros_like(o_ref)

    last_step = jnp.logical_and(j == nf - 1, i == pl.num_programs(1) - 1)
    drain_parity = lax.rem(nf * nu, 2)

    @pl.when(jnp.logical_and(last_step, drain_parity == 0))
    def _():
        wait_rows(xa_ref, sems.at[0])

    @pl.when(jnp.logical_and(last_step, drain_parity == 1))
    def _():
        wait_rows(xb_ref, sems.at[1])


def _gate_up(hp, tok, blk_expert, n_used, w_gu, b_gu, layer):
    nb, _, tm = tok.shape
    half = hp.shape[1]
    D = 2 * half
    F = w_gu.shape[-1] // 2
    tf = _tile(F, 1024)
    nf = F // tf
    nxt = lambda i, nu: jnp.where(i + 1 < nu[0], i + 1, 0)
    smem_blk = lambda f: pl.BlockSpec((None, 1, tm), f, memory_space=pltpu.SMEM)
    return pl.pallas_call(
        functools.partial(_gate_up_kernel, nf=nf),
        grid_spec=pltpu.PrefetchScalarGridSpec(
            num_scalar_prefetch=2,
            grid=(nf, nb),
            in_specs=[smem_blk(lambda j, i, be, nu: (0, 0, 0)),
                      smem_blk(lambda j, i, be, nu: (nxt(i, nu), 0, 0)),
                      pl.BlockSpec(memory_space=pl.ANY),
                      pl.BlockSpec((None, None, D, tf), lambda j, i, be, nu: (layer, be[i], 0, j)),
                      pl.BlockSpec((None, None, D, tf), lambda j, i, be, nu: (layer, be[i], 0, nf + j)),
                      pl.BlockSpec((None, None, 1, tf), lambda j, i, be, nu: (layer, be[i], 0, j)),
                      pl.BlockSpec((None, None, 1, tf), lambda j, i, be, nu: (layer, be[i], 0, nf + j))],
            out_specs=pl.BlockSpec((tm, tf), lambda j, i, be, nu: (i, j)),
            scratch_shapes=[pltpu.VMEM((D, tf), BF16), pltpu.VMEM((D, tf), BF16),
                            pltpu.VMEM((tm, half), U32), pltpu.VMEM((tm, half), U32),
                            pltpu.SemaphoreType.DMA((2,))]),
        out_shape=jax.ShapeDtypeStruct((nb * tm, F), BF16),
        compiler_params=_params("arbitrary", "arbitrary"),
        name="moe_gate_up",
    )(blk_expert, n_used, tok, tok, hp, w_gu, w_gu, b_gu, b_gu)


def _down_kernel(be_ref, nu_ref, slotp_ref, slotc_ref, a_ref, w_ref, b_ref, yt_ref, wb_ref, ya_ref, yb_ref,
                 sems):
    i = pl.program_id(0)
    nu = nu_ref[0]
    nb = pl.num_programs(0)
    tm = ya_ref.shape[0]
    n_slots = yt_ref.shape[0]

    def scatter(slot_ref, y_ref, sem):
        for r in range(tm):
            pltpu.make_async_copy(y_ref.at[pl.ds(r, 1)], yt_ref.at[pl.ds(slot_ref[0, r], 1)], sem).start()

    def wait_rows(y_ref, sem):
        pltpu.make_async_copy(y_ref, yt_ref.at[pl.ds(0, tm)], sem).wait()

    @pl.when(_expert_changed(be_ref, i))
    def _():
        wb_ref[...] = w_ref[...].astype(BF16)

    @pl.when(i == 0)
    def _():
        ya_ref[...] = jnp.zeros_like(ya_ref)
        yb_ref[...] = jnp.zeros_like(yb_ref)
        pltpu.make_async_copy(ya_ref, yt_ref.at[pl.ds(n_slots - tm, tm)], sems.at[0]).start()

    def step(y_ref, sem, py_ref, psem, used):
        wait_rows(y_ref, sem)
        if used:
            a = a_ref[...]
        scatter(slotp_ref, py_ref, psem)
        if used:
            y = jnp.dot(a, wb_ref[...], preferred_element_type=F32) + b_ref[...]
            y_ref[...] = _pack_halves(y)
        else:
            y_ref[...] = jnp.zeros_like(y_ref)

        @pl.when(i == nb - 1)
        def _():
            scatter(slotc_ref, y_ref, sem)
            wait_rows(y_ref, sem)
            wait_rows(py_ref, psem)

    for parity, (y_ref, py_ref) in enumerate(((ya_ref, yb_ref), (yb_ref, ya_ref))):
        for used in (True, False):
            @pl.when(jnp.logical_and(lax.rem(i, 2) == parity, (i < nu) == used))
            def _(y_ref=y_ref, py_ref=py_ref, parity=parity, used=used):
                step(y_ref, sems.at[parity], py_ref, sems.at[1 - parity], used)


def _down(act, slot, blk_expert, n_used, w_down, b_down, layer):
    R, F = act.shape
    _, _, tm = slot.shape
    nb = R // tm
    n_slots = R + 2 * tm
    D = w_down.shape[-1]
    last = lambda i, nu: jnp.minimum(i, nu[0] - 1)
    smem_blk = lambda f: pl.BlockSpec((None, 1, tm), f, memory_space=pltpu.SMEM)
    return pl.pallas_call(
        _down_kernel,
        grid_spec=pltpu.PrefetchScalarGridSpec(
            num_scalar_prefetch=2,
            grid=(nb,),
            in_specs=[smem_blk(lambda i, be, nu: (i, 0, 0)),
                      smem_blk(lambda i, be, nu: (i + 1, 0, 0)),
                      pl.BlockSpec((tm, F), lambda i, be, nu: (last(i, nu), 0)),
                      pl.BlockSpec((None, None, F, D), lambda i, be, nu: (layer, be[i], 0, 0)),
                      pl.BlockSpec((None, None, 1, D), lambda i, be, nu: (layer, be[i], 0, 0))],
            out_specs=pl.BlockSpec(memory_space=pl.ANY),
            scratch_shapes=[pltpu.VMEM((F, D), BF16),
                            pltpu.VMEM((tm, D // 2), U32), pltpu.VMEM((tm, D // 2), U32),
                            pltpu.SemaphoreType.DMA((2,))]),
        out_shape=jax.ShapeDtypeStruct((n_slots, D // 2), U32),
        compiler_params=_params("arbitrary"),
        name="moe_down",
    )(blk_expert, n_used, slot, slot, act, w_down, b_down)


def _combine_kernel(*refs, tm, tc, ll, alpha):
    y_refs = refs[:TOP_K]
    gt_ref, x_ref, gate_ref, lng_ref, lnb_ref, *o_refs = refs[TOP_K:]
    i = pl.program_id(0)
    f_lo = f_hi = None
    for k, y_ref in enumerate(y_refs):
        lo, hi = _unpack_halves(y_ref[...])
        w = gt_ref[:, k:k + 1]
        f_lo = w * lo if k == 0 else f_lo + w * lo
        f_hi = w * hi if k == 0 else f_hi + w * hi
    f = jnp.concatenate([f_lo, f_hi], axis=1)
    g = _group(i, tm, tc, ll)
    r = alpha * x_ref[...] + gate_ref[pl.ds(g, 1), :] * f
    out = _norm(r) * lng_ref[...] + lnb_ref[...]
    if len(o_refs) == 1:
        o_refs[0][...] = out
    else:
        @pl.when(i * tm < tc)
        def _():
            o_refs[0][...] = out

        @pl.when(i * tm >= tc)
        def _():
            o_refs[1][...] = out


def _combine(yt, gates_t, x, mod, gate_chunk, ln_g, ln_b, *, tc, ll, alpha, split):
    T, D = x.shape
    tm = _row_tile(tc, ll, 256)
    nbc = tc // tm
    nbt = T // tm
    if split:
        out_specs = [pl.BlockSpec((tm, D), lambda i: (jnp.minimum(i, nbc - 1), 0)),
                     pl.BlockSpec((tm, D), lambda i: (jnp.maximum(i - nbc, 0), 0))]
        out_shape = [jax.ShapeDtypeStruct((tc, D), F32), jax.ShapeDtypeStruct((T - tc, D), F32)]
    else:
        out_specs = pl.BlockSpec((tm, D), lambda i: (i, 0))
        out_shape = jax.ShapeDtypeStruct((T, D), F32)
    return pl.pallas_call(
        functools.partial(_combine_kernel, tm=tm, tc=tc, ll=ll, alpha=alpha),
        grid=(T // tm,),
        in_specs=[pl.BlockSpec((tm, D // 2), functools.partial(lambda k, i: (k * nbt + i, 0), k))
                  for k in range(TOP_K)]
        + [pl.BlockSpec((tm, TOP_K), lambda i: (i, 0)),
           pl.BlockSpec((tm, D), lambda i: (i, 0)),
           pl.BlockSpec((MOD_ROWS, D), lambda i: (0, gate_chunk)),
           pl.BlockSpec((1, D), lambda i: (0, 0)),
           pl.BlockSpec((1, D), lambda i: (0, 0))],
        out_specs=out_specs,
        out_shape=out_shape,
        compiler_params=_params("arbitrary"),
        name="moe_combine_residual_ln",
    )(*[yt] * TOP_K, gates_t, x, mod, ln_g.reshape(1, D), ln_b.reshape(1, D))


def _moe_block(x, mod, layer, router_w, router_b, w_gu, b_gu, w_down, b_down, ln_g, ln_b, *, tc, ll, alpha,
               split):
    assert N_EXPERTS * x.shape[0] * TOP_K < 2 ** 31
    T, D = x.shape
    E = N_EXPERTS
    A = T * TOP_K
    tm = MOE_ROWS
    hp, idx, gates, cnt = _router(x, mod, 3, router_w, router_b, tc=tc, ll=ll)
    a_ids = jnp.arange(A, dtype=I32)
    a_sorted = jnp.sort(idx.reshape(A) * A + a_ids) % A
    a_sorted = jnp.concatenate([a_sorted, jnp.zeros((tm,), I32)])
    counts = cnt[:, 0].astype(I32)
    cum_incl = jnp.cumsum(counts)
    padded = (counts + tm - 1) // tm * tm
    pad_ends = jnp.cumsum(padded)
    nb = pl.cdiv(A + E * (tm - 1), tm)
    blk_start = jnp.arange(nb, dtype=I32) * tm
    blk_expert = jnp.minimum(jnp.sum(blk_start[:, None] >= pad_ends[None, :], axis=1), E - 1).astype(I32)
    n_used = (pad_ends[-1:] // tm).astype(I32)
    offset = blk_start - (pad_ends - padded)[blk_expert]
    n_valid = jnp.clip(counts[blk_expert] - offset, 0, tm)
    first = jnp.clip((cum_incl - counts)[blk_expert] + offset, 0, A)
    window = jax.vmap(lambda s: lax.dynamic_slice(a_sorted, (s,), (tm,)))(first)
    lane = jnp.arange(tm, dtype=I32)[None, :]
    valid = lane < n_valid[:, None]
    spare = A + blk_start[:, None] + lane - cum_incl[blk_expert][:, None]
    slot = jnp.concatenate([nb * tm + lane, jnp.where(valid, window, spare)]).reshape(nb + 1, 1, tm)
    tok = jnp.where(valid, window % T, 0).reshape(nb, 1, tm)
    E4 = b_gu.shape
    act = _gate_up(hp, tok, blk_expert, n_used, w_gu, b_gu.reshape(E4[0], E4[1], 1, E4[2]), layer)
    yt = _down(act, slot, blk_expert, n_used, w_down, b_down.reshape(b_down.shape[0], E, 1, D), layer)
    return _combine(yt, gates.T, x, mod, 5, ln_g, ln_b, tc=tc, ll=ll, alpha=alpha, split=split)


def kernel(x_prompt, x_sample, state_ret, c, c_ctx, ada_w, ada_b, ln_g, ln_b, ret_w_in, ret_w_out, ret_decay,
           gm_w_in, gm_b_in, gm_ln_g, gm_ln_b, gm_w_s, gm_b_s, gm_w_out, sc_w_in, sc_conv, sc_w_out,
           moe_router_w, moe_router_b, moe_w_gu, moe_b_gu, moe_w_down, moe_b_down):
    B, S, D = x_prompt.shape
    BL, SL, _ = x_sample.shape
    depth = ada_w.shape[0]
    assert 1 + BL <= MOD_ROWS
    assert S % RET_CHUNK == 0 and SL % RET_CHUNK == 0 and SL % GRID_W == 0
    tc, ll = B * S, SL
    alpha = (2.0 * depth) ** 0.25

    x = jnp.concatenate([x_prompt.reshape(tc, D), x_sample.reshape(BL * SL, D)], axis=0)
    cond = jnp.concatenate([c_ctx[None], c, jnp.zeros((MOD_ROWS - 1 - BL, D), F32)], axis=0)
    mods = _modulation(cond, ada_w, ada_b)
    states = None
    kw = dict(tc=tc, ll=ll)
    for i in range(depth):
        kind, j = i % N_MIXERS, i // N_MIXERS
        mod = mods[i]
        if kind == 0:
            a = _mm_mod(x, mod, 0, ret_w_in[j].astype(BF16), None, **kw)
            yf, yb, states = _retention(a, ret_decay[j], state_ret, j, states, n_ctx_seq=B, ctx_len=S, lat_len=SL)
            ys, w_out = [yf, yb], ret_w_out[j]
        elif kind == 1:
            z = _mm_mod(x, mod, 0, gm_w_in[j].astype(BF16), gm_b_in[j], gelu=True, **kw)
            ys, w_out = [_gmlp_core(z, gm_ln_g[j], gm_ln_b[j], gm_w_s[j], gm_b_s[j])], gm_w_out[j]
        else:
            a = _mm_mod(x, mod, 0, sc_w_in[j].astype(BF16), None, **kw)
            ys, w_out = [_sconv_core(a, sc_conv[j], p_ctx=S, p_lat=GRID_W, **kw)], sc_w_out[j]
        x = _out_ln(ys, w_out.astype(BF16), x, mod, 2, ln_g[i, 0], ln_b[i, 0], alpha=alpha, **kw)
        x = _moe_block(x, mod, i, moe_router_w[i], moe_router_b[i], moe_w_gu, moe_b_gu, moe_w_down, moe_b_down,
                       ln_g[i, 1], ln_b[i, 1], alpha=alpha, split=i == depth - 1, **kw)
    x_ctx, x_lat = x
    return (x_ctx.reshape(B, S, D), x_lat.reshape(BL, SL, D), states)
```

```python
import functools
import math

import jax
import jax.numpy as jnp
from jax import lax
from jax.experimental import pallas as pl
from jax.experimental.pallas import tpu as pltpu

F32 = jnp.float32
BF16 = jnp.bfloat16
I32 = jnp.int32
U32 = jnp.uint32

N_MIXERS = 3
RET_HEADS = 8
RET_CHUNK = 128
GM_GROUPS = 8
GM_CHUNK = 128
GRID_W = 64
N_EXPERTS = 32
TOP_K = 4
SWIGLU_LIMIT = 7.0
SWIGLU_ALPHA = 1.702
LN_EPS = 1e-5

MOD_ROWS = 8
VMEM_LIMIT_BYTES = 52 * 1024 * 1024
MOE_ROWS = 256
RET_HEADS_PER_STEP = 4

NT_DIMS = (((1,), (1,)), ((), ()))
TN_DIMS = (((0,), (0,)), ((), ()))


def _tile(n, pref):
    t = min(n, pref)
    while n % t:
        t //= 2
    return t


def _row_tile(tc, ll, pref):
    return _tile(math.gcd(tc, ll), pref)


def _params(*sem):
    return pltpu.CompilerParams(dimension_semantics=sem, vmem_limit_bytes=VMEM_LIMIT_BYTES)


def _group(i, tm, tc, ll):
    row0 = i * tm
    return jnp.where(row0 < tc, 0, 1 + (row0 - tc) // ll)


def _norm(r):
    mu = jnp.mean(r, axis=-1, keepdims=True)
    d = r - mu
    var = jnp.mean(d * d, axis=-1, keepdims=True)
    return d * lax.rsqrt(var + LN_EPS)


def _silu(x):
    return x * jax.nn.sigmoid(x)


def _pack_halves(x):
    n2 = x.shape[1] // 2
    lo = pltpu.bitcast(x[:, :n2].astype(BF16).astype(F32), U32)
    hi = pltpu.bitcast(x[:, n2:].astype(BF16).astype(F32), U32)
    return (lo >> 16) | (hi & jnp.uint32(0xFFFF0000))


def _unpack_halves(p):
    return pltpu.bitcast(p << 16, F32), pltpu.bitcast(p & jnp.uint32(0xFFFF0000), F32)


def _mod_kernel(c_ref, w_ref, b_ref, o_ref):
    s = _silu(c_ref[...]).astype(BF16)
    o_ref[0] = jnp.dot(s, w_ref[0].astype(BF16), preferred_element_type=F32) + b_ref[0]


def _modulation(cond, ada_w, ada_b):
    L, D, N = ada_w.shape
    tn = _tile(N, 1024)
    return pl.pallas_call(
        _mod_kernel,
        grid=(L, N // tn),
        in_specs=[pl.BlockSpec((MOD_ROWS, D), lambda l, j: (0, 0)),
                  pl.BlockSpec((1, D, tn), lambda l, j: (l, 0, j)),
                  pl.BlockSpec((1, 1, tn), lambda l, j: (l, 0, j))],
        out_specs=pl.BlockSpec((1, MOD_ROWS, tn), lambda l, j: (l, 0, j)),
        out_shape=jax.ShapeDtypeStruct((L, MOD_ROWS, N), F32),
        compiler_params=_params("parallel", "parallel"),
        name="adaln_modulation",
    )(cond, ada_w, ada_b.reshape(L, 1, N))


def _mm_mod_kernel(*refs, tm, tc, ll, has_bias, gelu):
    if has_bias:
        x_ref, sh_ref, sc_ref, w_ref, b_ref, o_ref, hb_ref = refs
    else:
        x_ref, sh_ref, sc_ref, w_ref, o_ref, hb_ref = refs
    i, j = pl.program_id(0), pl.program_id(1)

    @pl.when(j == 0)
    def _():
        g = _group(i, tm, tc, ll)
        h = x_ref[...] * (1.0 + sc_ref[pl.ds(g, 1), :]) + sh_ref[pl.ds(g, 1), :]
        hb_ref[...] = h.astype(BF16)

    acc = jnp.dot(hb_ref[...], w_ref[...], preferred_element_type=F32)
    if has_bias:
        acc = acc + b_ref[...]
    if gelu:
        acc = 0.5 * acc * (1.0 + lax.erf(acc * (2.0 ** -0.5)))
    o_ref[...] = acc.astype(o_ref.dtype)


def _mm_mod(x, mod, shift_chunk, w, bias, *, tc, ll, gelu=False):
    T, D = x.shape
    N = w.shape[1]
    tm = _row_tile(tc, ll, 1024)
    tn = _tile(N, 1024)
    in_specs = [pl.BlockSpec((tm, D), lambda i, j: (i, 0)),
                pl.BlockSpec((MOD_ROWS, D), lambda i, j: (0, shift_chunk)),
                pl.BlockSpec((MOD_ROWS, D), lambda i, j: (0, shift_chunk + 1)),
                pl.BlockSpec((D, tn), lambda i, j: (0, j))]
    args = [x, mod, mod, w]
    if bias is not None:
        in_specs.append(pl.BlockSpec((1, tn), lambda i, j: (0, j)))
        args.append(bias.reshape(1, N))
    return pl.pallas_call(
        functools.partial(_mm_mod_kernel, tm=tm, tc=tc, ll=ll, has_bias=bias is not None, gelu=gelu),
        grid=(T // tm, N // tn),
        in_specs=in_specs,
        out_specs=pl.BlockSpec((tm, tn), lambda i, j: (i, j)),
        out_shape=jax.ShapeDtypeStruct((T, N), BF16),
        scratch_shapes=[pltpu.VMEM((tm, D), BF16)],
        compiler_params=_params("parallel", "arbitrary"),
        name="modulate_in_proj",
    )(*args)


def _out_ln_kernel(*refs, n_y, nk, tm, tc, ll, alpha):
    y_refs = refs[:n_y]
    w_ref, x_ref, gate_ref, lng_ref, lnb_ref, o_ref, acc_ref = refs[n_y:]
    i, k = pl.program_id(0), pl.program_id(1)

    @pl.when(k == 0)
    def _():
        acc_ref[...] = jnp.zeros_like(acc_ref)

    if n_y == 1:
        y = y_refs[0][...]
    else:
        y = (y_refs[0][...].astype(F32) + y_refs[1][...].astype(F32)).astype(BF16)
    acc_ref[...] += jnp.dot(y, w_ref[...], preferred_element_type=F32)

    @pl.when(k == nk - 1)
    def _():
        g = _group(i, tm, tc, ll)
        r = alpha * x_ref[...] + gate_ref[pl.ds(g, 1), :] * acc_ref[...]
        o_ref[...] = _norm(r) * lng_ref[...] + lnb_ref[...]


def _out_ln(ys, w, x, mod, gate_chunk, ln_g, ln_b, *, tc, ll, alpha):
    T, D = x.shape
    K = w.shape[0]
    tm = _row_tile(tc, ll, 512)
    tk = _tile(K, 2048)
    nk = K // tk
    in_specs = [pl.BlockSpec((tm, tk), lambda i, k: (i, k)) for _ in ys]
    in_specs += [pl.BlockSpec((tk, D), lambda i, k: (k, 0)),
                 pl.BlockSpec((tm, D), lambda i, k: (i, 0)),
                 pl.BlockSpec((MOD_ROWS, D), lambda i, k: (0, gate_chunk)),
                 pl.BlockSpec((1, D), lambda i, k: (0, 0)),
                 pl.BlockSpec((1, D), lambda i, k: (0, 0))]
    return pl.pallas_call(
        functools.partial(_out_ln_kernel, n_y=len(ys), nk=nk, tm=tm, tc=tc, ll=ll, alpha=alpha),
        grid=(T // tm, nk),
        in_specs=in_specs,
        out_specs=pl.BlockSpec((tm, D), lambda i, k: (i, 0)),
        out_shape=jax.ShapeDtypeStruct((T, D), F32),
        scratch_shapes=[pltpu.VMEM((tm, D), F32)],
        compiler_params=_params("parallel", "arbitrary"),
        name="out_proj_residual_ln",
    )(*ys, w, x, mod, ln_g.reshape(1, D), ln_b.reshape(1, D))


def _ret_item(n, lay):
    ncx, nc_c, nc_l = lay
    is_ctx = n < ncx
    m = jnp.where(is_ctx, n, n - ncx)
    seq = jnp.where(is_ctx, m // nc_c, m // nc_l)
    c = jnp.where(is_ctx, m % nc_c, m % nc_l)
    nc = jnp.where(is_ctx, nc_c, nc_l)
    return is_ctx, seq, c, nc, n - c


def _ret_kernel(*refs, n_prev, lay, scale):
    dec_ref, qf, kf, vf, gf, qb, kb, vb, gb, s0_ref = refs[:10]
    prev_refs = refs[10:10 + n_prev]
    yf_ref, yb_ref, out_ref, st_ref = refs[10 + n_prev:]
    hb, dk, dv = st_ref.shape[1:]
    h0 = pl.program_id(0) * hb
    is_ctx, _, c, nc, _ = _ret_item(pl.program_id(1), lay)
    C = RET_CHUNK

    @pl.when(c == 0)
    def _():
        st_ref[...] = jnp.where(is_ctx, 0.0, s0_ref[...])

    ri = lax.broadcasted_iota(I32, (C, C), 0)
    ci = lax.broadcasted_iota(I32, (C, C), 1)
    row = lax.broadcasted_iota(I32, (C, 1), 0).astype(F32)
    dirs = ((qf, kf, vf, gf, yf_ref), (qb, kb, vb, gb, yb_ref))
    for d, (q_ref, k_ref, v_ref, g_ref, y_ref) in enumerate(dirs):
        rel = ri - ci if d == 0 else ci - ri
        pos = row if d == 0 else (C - 1.0) - row
        for hh in range(hb):
            kc, vc = slice(hh * dk, (hh + 1) * dk), slice(hh * dv, (hh + 1) * dv)
            lg = -jnp.exp(jnp.full((1, 1), dec_ref[d, h0 + hh], F32))
            dmask = jnp.where(rel >= 0, jnp.exp(lg * jnp.maximum(rel, 0).astype(F32)), 0.0) * scale
            xi = jnp.exp(lg * (pos + 1.0))
            zeta = jnp.exp(lg * ((C - 1.0) - pos)) * scale
            q, k, v = q_ref[:, kc], k_ref[:, kc], v_ref[:, vc]
            s = st_ref[d, hh]
            scores = lax.dot_general(q, k, NT_DIMS, preferred_element_type=F32) * dmask
            o = (jnp.dot(scores.astype(BF16), v, preferred_element_type=F32)
                 + jnp.dot(q, s.astype(BF16), preferred_element_type=F32) * xi)
            kz = (k.astype(F32) * zeta).astype(BF16)
            st_ref[d, hh] = (jnp.exp(lg * float(C)) * s
                             + lax.dot_general(kz, v, TN_DIMS, preferred_element_type=F32))
            y_ref[:, vc] = (_silu(g_ref[:, vc].astype(F32)) * _norm(o)).astype(y_ref.dtype)

    @pl.when(jnp.logical_and(is_ctx, c == nc - 1))
    def _():
        for p, prev_ref in enumerate(prev_refs):
            out_ref[p] = prev_ref[...]
        out_ref[n_prev] = st_ref[...]


def _retention(a, decay, s0, s0_layer, prev_states, *, n_ctx_seq, ctx_len, lat_len):
    T = a.shape[0]
    H = RET_HEADS
    DK, DV = s0.shape[-2], s0.shape[-1]
    C = RET_CHUNK
    lay = (n_ctx_seq * ctx_len // C, ctx_len // C, lat_len // C)
    n_prev = 0 if prev_states is None else prev_states.shape[1]

    def fwd(n):
        return n

    def bwd(n):
        _, _, c, nc, first = _ret_item(n, lay)
        return first + (nc - 1 - c)

    def lat_seq(n):
        is_ctx, seq, _, _, _ = _ret_item(n, lay)
        return jnp.where(is_ctx, 0, seq)

    def ctx_seq(n):
        is_ctx, seq, _, _, _ = _ret_item(n, lay)
        return jnp.where(is_ctx, seq, n_ctx_seq - 1)

    hb = RET_HEADS_PER_STEP
    nh = H // hb

    def col_specs(rowf):
        return [pl.BlockSpec((C, hb * DK), lambda h, n: (rowf(n), h)),
                pl.BlockSpec((C, hb * DK), lambda h, n: (rowf(n), nh + h)),
                pl.BlockSpec((C, hb * DV), lambda h, n: (rowf(n), nh + h))]

    def state_spec(layers, seqf):
        return pl.BlockSpec((None, layers, 2, hb, DK, DV), lambda h, n: (seqf(n), 0, 0, h, 0, 0))

    in_specs = [pl.BlockSpec(memory_space=pltpu.SMEM)]
    in_specs += col_specs(fwd) + [pl.BlockSpec((C, hb * DV), lambda h, n: (fwd(n), 2 * nh + h))]
    in_specs += col_specs(bwd) + [pl.BlockSpec((C, hb * DV), lambda h, n: (bwd(n), 3 * nh + h))]
    in_specs.append(pl.BlockSpec((None, None, 2, hb, DK, DV), lambda h, n: (lat_seq(n), s0_layer, 0, h, 0, 0)))
    args = [decay] + [a] * 8 + [s0]
    if n_prev:
        in_specs.append(state_spec(n_prev, ctx_seq))
        args.append(prev_states)

    def kern(*refs):
        if n_prev:
            refs = refs[:10] + tuple(refs[10].at[p] for p in range(n_prev)) + refs[11:]
        _ret_kernel(*refs, n_prev=n_prev, lay=lay, scale=float(DK) ** -0.5)

    V = H * DV
    return pl.pallas_call(
        kern,
        grid=(nh, T // C),
        in_specs=in_specs,
        out_specs=[pl.BlockSpec((C, hb * DV), lambda h, n: (fwd(n), h)),
                   pl.BlockSpec((C, hb * DV), lambda h, n: (bwd(n), h)),
                   state_spec(n_prev + 1, ctx_seq)],
        out_shape=[jax.ShapeDtypeStruct((T, V), BF16), jax.ShapeDtypeStruct((T, V), BF16),
                   jax.ShapeDtypeStruct((n_ctx_seq, n_prev + 1, 2, H, DK, DV), F32)],
        scratch_shapes=[pltpu.VMEM((2, hb, DK, DV), F32)],
        compiler_params=_params("parallel", "arbitrary"),
        name="retention_core",
    )(*args)


def _gmlp_kernel(u_ref, v_ref, lng_ref, lnb_ref, ws_ref, bs_ref, o_ref):
    vn = (_norm(v_ref[...].astype(F32)) * lng_ref[...] + lnb_ref[...]).astype(BF16)
    gw = vn.shape[1] // GM_GROUPS
    for g in range(GM_GROUPS):
        cols = slice(g * gw, (g + 1) * gw)
        mixed = jnp.dot(ws_ref[g], vn[:, cols], preferred_element_type=F32) + bs_ref[:, g:g + 1]
        o_ref[:, cols] = (u_ref[:, cols].astype(F32) * mixed).astype(o_ref.dtype)


def _gmlp_core(z, ln_g, ln_b, w_s, b_s):
    T, W2 = z.shape
    W = W2 // 2
    C = GM_CHUNK
    return pl.pallas_call(
        _gmlp_kernel,
        grid=(T // C,),
        in_specs=[pl.BlockSpec((C, W), lambda n: (n, 0)),
                  pl.BlockSpec((C, W), lambda n: (n, 1)),
                  pl.BlockSpec((1, W), lambda n: (0, 0)),
                  pl.BlockSpec((1, W), lambda n: (0, 0)),
                  pl.BlockSpec((GM_GROUPS, C, C), lambda n: (0, 0, 0)),
                  pl.BlockSpec((C, GM_GROUPS), lambda n: (0, 0))],
        out_specs=pl.BlockSpec((C, W), lambda n: (n, 0)),
        out_shape=jax.ShapeDtypeStruct((T, W), BF16),
        compiler_params=_params("parallel"),
        name="gmlp_spatial_gate",
    )(z, z, ln_g.reshape(1, W), ln_b.reshape(1, W), w_s.astype(BF16), b_s.T)


def _sconv_kernel(b_ref, c_ref, x_ref, w_ref, o_ref, *, tm, tc, p_ctx, p_lat):
    i = pl.program_id(0)
    t = c_ref[...].astype(F32) * x_ref[...].astype(F32)
    row = lax.broadcasted_iota(I32, (tm, 1), 0)
    is_ctx = i * tm < tc
    pos = jnp.where(is_ctx, row % p_ctx, row % p_lat)
    t_prev = jnp.where(pos == 0, 0.0, pltpu.roll(t, 1, 0))
    t_next = jnp.where(pos == jnp.where(is_ctx, p_ctx - 1, p_lat - 1), 0.0, pltpu.roll(t, tm - 1, 0))
    conv = t_prev * w_ref[0:1, :] + t * w_ref[1:2, :] + t_next * w_ref[2:3, :]
    o_ref[...] = (b_ref[...].astype(F32) * conv).astype(o_ref.dtype)


def _sconv_core(a, conv_w, *, tc, ll, p_ctx, p_lat):
    T, D3 = a.shape
    D = D3 // 3
    tm = _row_tile(tc, ll, 512)
    assert tm % p_ctx == 0 and tm % p_lat == 0
    tn = _tile(D, 512)
    nj = D // tn
    return pl.pallas_call(
        functools.partial(_sconv_kernel, tm=tm, tc=tc, p_ctx=p_ctx, p_lat=p_lat),
        grid=(T // tm, nj),
        in_specs=[pl.BlockSpec((tm, tn), lambda i, j: (i, j)),
                  pl.BlockSpec((tm, tn), lambda i, j: (i, nj + j)),
                  pl.BlockSpec((tm, tn), lambda i, j: (i, 2 * nj + j)),
                  pl.BlockSpec((3, tn), lambda i, j: (0, j))],
        out_specs=pl.BlockSpec((tm, tn), lambda i, j: (i, j)),
        out_shape=jax.ShapeDtypeStruct((T, D), BF16),
        compiler_params=_params("parallel", "parallel"),
        name="short_conv_gate",
    )(a, a, a, conv_w)


def _router_kernel(x_ref, sh_ref, sc_ref, wt_ref, b_ref, hp_ref, idx_ref, gate_ref, cnt_ref, carry_ref,
                   *, tm, tc, ll):
    i = pl.program_id(0)
    E = N_EXPERTS

    @pl.when(i == 0)
    def _():
        carry_ref[...] = jnp.zeros_like(carry_ref)

    g = _group(i, tm, tc, ll)
    h = x_ref[...] * (1.0 + sc_ref[pl.ds(g, 1), :]) + sh_ref[pl.ds(g, 1), :]
    hp_ref[...] = _pack_halves(h)
    hh = h.astype(BF16)
    hl = (h - hh.astype(F32)).astype(BF16)
    w = wt_ref[...]
    wh = w.astype(BF16)
    wl = (w - wh.astype(F32)).astype(BF16)
    logits = (lax.dot_general(wh, hh, NT_DIMS, preferred_element_type=F32)
              + lax.dot_general(wh, hl, NT_DIMS, preferred_element_type=F32)
              + lax.dot_general(wl, hh, NT_DIMS, preferred_element_type=F32)) + b_ref[...]
    e_iota = lax.broadcasted_iota(I32, (E, tm), 0)
    count = carry_ref[...]
    vals = logits
    tops = []
    for k in range(TOP_K):
        m = jnp.max(vals, axis=0, keepdims=True)
        idx = jnp.min(jnp.where(vals == m, e_iota, E), axis=0, keepdims=True)
        hit = e_iota == idx
        count = count + jnp.sum(jnp.where(hit, 1.0, 0.0), axis=1, keepdims=True)
        idx_ref[k:k + 1, :] = idx
        tops.append(m)
        vals = jnp.where(hit, -jnp.inf, vals)
    carry_ref[...] = count
    cnt_ref[...] = count
    exps = [jnp.exp(t - tops[0]) for t in tops]
    den = exps[0]
    for e in exps[1:]:
        den = den + e
    for k in range(TOP_K):
        gate_ref[k:k + 1, :] = exps[k] / den


def _router(x, mod, shift_chunk, router_w, router_b, *, tc, ll):
    T, D = x.shape
    E = N_EXPERTS
    tm = _row_tile(tc, ll, 256)
    row = lambda i: (0, i)
    return pl.pallas_call(
        functools.partial(_router_kernel, tm=tm, tc=tc, ll=ll),
        grid=(T // tm,),
        in_specs=[pl.BlockSpec((tm, D), lambda i: (i, 0)),
                  pl.BlockSpec((MOD_ROWS, D), lambda i: (0, shift_chunk)),
                  pl.BlockSpec((MOD_ROWS, D), lambda i: (0, shift_chunk + 1)),
                  pl.BlockSpec((E, D), lambda i: (0, 0)),
                  pl.BlockSpec((E, 1), lambda i: (0, 0))],
        out_specs=[pl.BlockSpec((tm, D // 2), lambda i: (i, 0)),
                   pl.BlockSpec((TOP_K, tm), row),
                   pl.BlockSpec((TOP_K, tm), row),
                   pl.BlockSpec((E, 1), lambda i: (0, 0))],
        out_shape=[jax.ShapeDtypeStruct((T, D // 2), U32),
                   jax.ShapeDtypeStruct((TOP_K, T), I32),
                   jax.ShapeDtypeStruct((TOP_K, T), F32),
                   jax.ShapeDtypeStruct((E, 1), F32)],
        scratch_shapes=[pltpu.VMEM((E, 1), F32)],
        compiler_params=_params("arbitrary"),
        name="moe_router",
    )(x, mod, mod, router_w.T, router_b.reshape(E, 1))


def _expert_changed(be_ref, i):
    return jnp.logical_or(i == 0, be_ref[i] != be_ref[jnp.maximum(i - 1, 0)])


def _gather_kernel(nu_ref, tok_ref, hp_ref, o_ref, sem):
    i = pl.program_id(0)
    tm = o_ref.shape[0]

    @pl.when(i < nu_ref[0])
    def _():
        def issue(r, carry):
            pltpu.make_async_copy(hp_ref.at[pl.ds(tok_ref[0, r], 1)], o_ref.at[pl.ds(r, 1)], sem).start()
            return carry

        lax.fori_loop(0, tm, issue, 0, unroll=8)
        pltpu.make_async_copy(hp_ref.at[pl.ds(0, tm)], o_ref, sem).wait()

    @pl.when(i >= nu_ref[0])
    def _():
        o_ref[...] = jnp.zeros_like(o_ref)


def _gather_rows(hp, tok, n_used):
    nb, _, tm = tok.shape
    half = hp.shape[1]
    return pl.pallas_call(
        _gather_kernel,
        grid_spec=pltpu.PrefetchScalarGridSpec(
            num_scalar_prefetch=1,
            grid=(nb,),
            in_specs=[pl.BlockSpec((None, 1, tm), lambda i, nu: (i, 0, 0), memory_space=pltpu.SMEM),
                      pl.BlockSpec(memory_space=pl.ANY)],
            out_specs=pl.BlockSpec((tm, half), lambda i, nu: (i, 0)),
            scratch_shapes=[pltpu.SemaphoreType.DMA(())]),
        out_shape=jax.ShapeDtypeStruct((nb * tm, half), hp.dtype),
        compiler_params=_params("arbitrary"),
        name="moe_gather",
    )(n_used, tok, hp)


def _gate_up_kernel(be_ref, nu_ref, x_ref, wg_ref, wu_ref, bg_ref, bu_ref, o_ref, wgb_ref, wub_ref):
    i = pl.program_id(1)

    @pl.when(_expert_changed(be_ref, i))
    def _():
        wgb_ref[...] = wg_ref[...].astype(BF16)
        wub_ref[...] = wu_ref[...].astype(BF16)

    @pl.when(i < nu_ref[0])
    def _():
        half = x_ref.shape[1]
        lo, hi = _unpack_halves(x_ref[...])
        lo, hi = lo.astype(BF16), hi.astype(BF16)

        def proj(w_ref, b_ref):
            return (jnp.dot(lo, w_ref[:half, :], preferred_element_type=F32)
                    + jnp.dot(hi, w_ref[half:, :], preferred_element_type=F32) + b_ref[...])

        gate = jnp.minimum(proj(wgb_ref, bg_ref), SWIGLU_LIMIT)
        up = jnp.clip(proj(wub_ref, bu_ref), -SWIGLU_LIMIT, SWIGLU_LIMIT)
        act = (up + 1.0) * (gate * jax.nn.sigmoid(SWIGLU_ALPHA * gate))
        o_ref[...] = act.astype(o_ref.dtype)

    @pl.when(i >= nu_ref[0])
    def _():
        o_ref[...] = jnp.zeros_like(o_ref)


def _gate_up(xs, blk_expert, n_used, w_gu, b_gu, layer):
    R, half = xs.shape
    tm = MOE_ROWS
    nb = R // tm
    D = 2 * half
    F = w_gu.shape[-1] // 2
    tf = _tile(F, 1024)
    nf = F // tf
    last = lambda i, nu: jnp.minimum(i, nu[0] - 1)
    return pl.pallas_call(
        _gate_up_kernel,
        grid_spec=pltpu.PrefetchScalarGridSpec(
            num_scalar_prefetch=2,
            grid=(nf, nb),
            in_specs=[pl.BlockSpec((tm, half), lambda j, i, be, nu: (last(i, nu), 0)),
                      pl.BlockSpec((None, None, D, tf), lambda j, i, be, nu: (layer, be[i], 0, j)),
                      pl.BlockSpec((None, None, D, tf), lambda j, i, be, nu: (layer, be[i], 0, nf + j)),
                      pl.BlockSpec((None, None, 1, tf), lambda j, i, be, nu: (layer, be[i], 0, j)),
                      pl.BlockSpec((None, None, 1, tf), lambda j, i, be, nu: (layer, be[i], 0, nf + j))],
            out_specs=pl.BlockSpec((tm, tf), lambda j, i, be, nu: (i, j)),
            scratch_shapes=[pltpu.VMEM((D, tf), BF16), pltpu.VMEM((D, tf), BF16)]),
        out_shape=jax.ShapeDtypeStruct((R, F), BF16),
        compiler_params=_params("arbitrary", "arbitrary"),
        name="moe_gate_up",
    )(blk_expert, n_used, xs, w_gu, w_gu, b_gu, b_gu)


def _down_kernel(be_ref, nu_ref, slotp_ref, slotc_ref, a_ref, w_ref, b_ref, yt_ref, wb_ref, ya_ref, yb_ref,
                 sems):
    i = pl.program_id(0)
    nu = nu_ref[0]
    nb = pl.num_programs(0)
    tm = ya_ref.shape[0]
    n_slots = yt_ref.shape[0]

    def scatter(slot_ref, y_ref, sem):
        for r in range(tm):
            pltpu.make_async_copy(y_ref.at[pl.ds(r, 1)], yt_ref.at[pl.ds(slot_ref[0, r], 1)], sem).start()

    def wait_rows(y_ref, sem):
        pltpu.make_async_copy(y_ref, yt_ref.at[pl.ds(0, tm)], sem).wait()

    @pl.when(_expert_changed(be_ref, i))
    def _():
        wb_ref[...] = w_ref[...].astype(BF16)

    @pl.when(i == 0)
    def _():
        ya_ref[...] = jnp.zeros_like(ya_ref)
        yb_ref[...] = jnp.zeros_like(yb_ref)
        pltpu.make_async_copy(ya_ref, yt_ref.at[pl.ds(n_slots - tm, tm)], sems.at[0]).start()

    def step(y_ref, sem, py_ref, psem, used):
        wait_rows(y_ref, sem)
        if used:
            a = a_ref[...]
        scatter(slotp_ref, py_ref, psem)
        if used:
            y = jnp.dot(a, wb_ref[...], preferred_element_type=F32) + b_ref[...]
            y_ref[...] = _pack_halves(y)
        else:
            y_ref[...] = jnp.zeros_like(y_ref)

        @pl.when(i == nb - 1)
        def _():
            scatter(slotc_ref, y_ref, sem)
            wait_rows(y_ref, sem)
            wait_rows(py_ref, psem)

    for parity, (y_ref, py_ref) in enumerate(((ya_ref, yb_ref), (yb_ref, ya_ref))):
        for used in (True, False):
            @pl.when(jnp.logical_and(lax.rem(i, 2) == parity, (i < nu) == used))
            def _(y_ref=y_ref, py_ref=py_ref, parity=parity, used=used):
                step(y_ref, sems.at[parity], py_ref, sems.at[1 - parity], used)


def _down(act, slot, blk_expert, n_used, w_down, b_down, layer):
    R, F = act.shape
    _, _, tm = slot.shape
    nb = R // tm
    n_slots = R + 2 * tm
    D = w_down.shape[-1]
    last = lambda i, nu: jnp.minimum(i, nu[0] - 1)
    smem_blk = lambda f: pl.BlockSpec((None, 1, tm), f, memory_space=pltpu.SMEM)
    return pl.pallas_call(
        _down_kernel,
        grid_spec=pltpu.PrefetchScalarGridSpec(
            num_scalar_prefetch=2,
            grid=(nb,),
            in_specs=[smem_blk(lambda i, be, nu: (i, 0, 0)),
                      smem_blk(lambda i, be, nu: (i + 1, 0, 0)),
                      pl.BlockSpec((tm, F), lambda i, be, nu: (last(i, nu), 0)),
                      pl.BlockSpec((None, None, F, D), lambda i, be, nu: (layer, be[i], 0, 0)),
                      pl.BlockSpec((None, None, 1, D), lambda i, be, nu: (layer, be[i], 0, 0))],
            out_specs=pl.BlockSpec(memory_space=pl.ANY),
            scratch_shapes=[pltpu.VMEM((F, D), BF16),
                            pltpu.VMEM((tm, D // 2), U32), pltpu.VMEM((tm, D // 2), U32),
                            pltpu.SemaphoreType.DMA((2,))]),
        out_shape=jax.ShapeDtypeStruct((n_slots, D // 2), U32),
        compiler_params=_params("arbitrary"),
        name="moe_down",
    )(blk_expert, n_used, slot, slot, act, w_down, b_down)


def _combine_kernel(*refs, tm, tc, ll, alpha):
    y_refs = refs[:TOP_K]
    gt_ref, x_ref, gate_ref, lng_ref, lnb_ref, *o_refs = refs[TOP_K:]
    i = pl.program_id(0)
    f_lo = f_hi = None
    for k, y_ref in enumerate(y_refs):
        lo, hi = _unpack_halves(y_ref[...])
        w = gt_ref[:, k:k + 1]
        f_lo = w * lo if k == 0 else f_lo + w * lo
        f_hi = w * hi if k == 0 else f_hi + w * hi
    f = jnp.concatenate([f_lo, f_hi], axis=1)
    g = _group(i, tm, tc, ll)
    r = alpha * x_ref[...] + gate_ref[pl.ds(g, 1), :] * f
    out = _norm(r) * lng_ref[...] + lnb_ref[...]
    if len(o_refs) == 1:
        o_refs[0][...] = out
    else:
        @pl.when(i * tm < tc)
        def _():
            o_refs[0][...] = out

        @pl.when(i * tm >= tc)
        def _():
            o_refs[1][...] = out


def _combine(yt, gates_t, x, mod, gate_chunk, ln_g, ln_b, *, tc, ll, alpha, split):
    T, D = x.shape
    tm = _row_tile(tc, ll, 256)
    nbc = tc // tm
    nbt = T // tm
    if split:
        out_specs = [pl.BlockSpec((tm, D), lambda i: (jnp.minimum(i, nbc - 1), 0)),
                     pl.BlockSpec((tm, D), lambda i: (jnp.maximum(i - nbc, 0), 0))]
        out_shape = [jax.ShapeDtypeStruct((tc, D), F32), jax.ShapeDtypeStruct((T - tc, D), F32)]
    else:
        out_specs = pl.BlockSpec((tm, D), lambda i: (i, 0))
        out_shape = jax.ShapeDtypeStruct((T, D), F32)
    return pl.pallas_call(
        functools.partial(_combine_kernel, tm=tm, tc=tc, ll=ll, alpha=alpha),
        grid=(T // tm,),
        in_specs=[pl.BlockSpec((tm, D // 2), functools.partial(lambda k, i: (k * nbt + i, 0), k))
                  for k in range(TOP_K)]
        + [pl.BlockSpec((tm, TOP_K), lambda i: (i, 0)),
           pl.BlockSpec((tm, D), lambda i: (i, 0)),
           pl.BlockSpec((MOD_ROWS, D), lambda i: (0, gate_chunk)),
           pl.BlockSpec((1, D), lambda i: (0, 0)),
           pl.BlockSpec((1, D), lambda i: (0, 0))],
        out_specs=out_specs,
        out_shape=out_shape,
        compiler_params=_params("arbitrary"),
        name="moe_combine_residual_ln",
    )(*[yt] * TOP_K, gates_t, x, mod, ln_g.reshape(1, D), ln_b.reshape(1, D))


def _moe_block(x, mod, layer, router_w, router_b, w_gu, b_gu, w_down, b_down, ln_g, ln_b, *, tc, ll, alpha,
               split):
    T, D = x.shape
    E = N_EXPERTS
    A = T * TOP_K
    tm = MOE_ROWS
    nb = pl.cdiv(A + E * (tm - 1), tm)
    n_rows = nb * tm
    big = pl.next_power_of_2(n_rows)
    assert (E + 1) * big < 2 ** 31
    hp, idx, gates, cnt = _router(x, mod, 3, router_w, router_b, tc=tc, ll=ll)
    counts = cnt[:, 0].astype(I32)
    padded = (counts + tm - 1) // tm * tm
    pad_ends = jnp.cumsum(padded)
    blk_start = jnp.arange(nb, dtype=I32) * tm
    blk_expert = jnp.minimum(jnp.sum(blk_start[:, None] >= pad_ends[None, :], axis=1), E - 1).astype(I32)
    n_used = (pad_ends[-1:] // tm).astype(I32)
    a_ids = jnp.arange(A, dtype=I32)
    f_ids = jnp.arange(n_rows - A, dtype=I32)
    f_expert = jnp.sum(f_ids[:, None] >= jnp.cumsum(padded - counts)[None, :], axis=1).astype(I32)
    keys = jnp.concatenate([idx.reshape(A) * big + a_ids, f_expert * big + A + f_ids])
    row_id = jnp.sort(keys) & (big - 1)
    slot = jnp.concatenate([n_rows + jnp.arange(tm, dtype=I32), row_id]).reshape(nb + 1, 1, tm)
    tok = jnp.where(row_id < A, row_id % T, 0).reshape(nb, 1, tm)
    E4 = b_gu.shape
    xs = _gather_rows(hp, tok, n_used)
    act = _gate_up(xs, blk_expert, n_used, w_gu, b_gu.reshape(E4[0], E4[1], 1, E4[2]), layer)
    yt = _down(act, slot, blk_expert, n_used, w_down, b_down.reshape(b_down.shape[0], E, 1, D), layer)
    return _combine(yt, gates.T, x, mod, 5, ln_g, ln_b, tc=tc, ll=ll, alpha=alpha, split=split)


def kernel(x_prompt, x_sample, state_ret, c, c_ctx, ada_w, ada_b, ln_g, ln_b, ret_w_in, ret_w_out, ret_decay,
           gm_w_in, gm_b_in, gm_ln_g, gm_ln_b, gm_w_s, gm_b_s, gm_w_out, sc_w_in, sc_conv, sc_w_out,
           moe_router_w, moe_router_b, moe_w_gu, moe_b_gu, moe_w_down, moe_b_down):
    B, S, D = x_prompt.shape
    BL, SL, _ = x_sample.shape
    depth = ada_w.shape[0]
    assert 1 + BL <= MOD_ROWS
    assert S % RET_CHUNK == 0 and SL % RET_CHUNK == 0 and SL % GRID_W == 0
    tc, ll = B * S, SL
    alpha = (2.0 * depth) ** 0.25

    x = jnp.concatenate([x_prompt.reshape(tc, D), x_sample.reshape(BL * SL, D)], axis=0)
    cond = jnp.concatenate([c_ctx[None], c, jnp.zeros((MOD_ROWS - 1 - BL, D), F32)], axis=0)
    mods = _modulation(cond, ada_w, ada_b)
    states = None
    kw = dict(tc=tc, ll=ll)
    for i in range(depth):
        kind, j = i % N_MIXERS, i // N_MIXERS
        mod = mods[i]
        if kind == 0:
            a = _mm_mod(x, mod, 0, ret_w_in[j].astype(BF16), None, **kw)
            yf, yb, states = _retention(a, ret_decay[j], state_ret, j, states, n_ctx_seq=B, ctx_len=S, lat_len=SL)
            ys, w_out = [yf, yb], ret_w_out[j]
        elif kind == 1:
            z = _mm_mod(x, mod, 0, gm_w_in[j].astype(BF16), gm_b_in[j], gelu=True, **kw)
            ys, w_out = [_gmlp_core(z, gm_ln_g[j], gm_ln_b[j], gm_w_s[j], gm_b_s[j])], gm_w_out[j]
        else:
            a = _mm_mod(x, mod, 0, sc_w_in[j].astype(BF16), None, **kw)
            ys, w_out = [_sconv_core(a, sc_conv[j], p_ctx=S, p_lat=GRID_W, **kw)], sc_w_out[j]
        x = _out_ln(ys, w_out.astype(BF16), x, mod, 2, ln_g[i, 0], ln_b[i, 0], alpha=alpha, **kw)
        x = _moe_block(x, mod, i, moe_router_w[i], moe_router_b[i], moe_w_gu, moe_b_gu, moe_w_down, moe_b_down,
                       ln_g[i, 1], ln_b[i, 1], alpha=alpha, split=i == depth - 1, **kw)
    x_ctx, x_lat = x
    return (x_ctx.reshape(B, S, D), x_lat.reshape(BL, SL, D), states)
```

```python
import functools
import math

import jax
import jax.numpy as jnp
from jax import lax
from jax.experimental import pallas as pl
from jax.experimental.pallas import tpu as pltpu

F32 = jnp.float32
BF16 = jnp.bfloat16
I32 = jnp.int32
U32 = jnp.uint32

N_MIXERS = 3
RET_HEADS = 8
RET_CHUNK = 128
GM_GROUPS = 8
GM_CHUNK = 128
GRID_W = 64
N_EXPERTS = 32
TOP_K = 4
SWIGLU_LIMIT = 7.0
SWIGLU_ALPHA = 1.702
LN_EPS = 1e-5

MOD_ROWS = 8
VMEM_LIMIT_BYTES = 52 * 1024 * 1024
MOE_ROWS = 256
RET_HEADS_PER_STEP = 4

NT_DIMS = (((1,), (1,)), ((), ()))
TN_DIMS = (((0,), (0,)), ((), ()))


def _tile(n, pref):
    t = min(n, pref)
    while n % t:
        t //= 2
    return t


def _row_tile(tc, ll, pref):
    return _tile(math.gcd(tc, ll), pref)


def _params(*sem):
    return pltpu.CompilerParams(dimension_semantics=sem, vmem_limit_bytes=VMEM_LIMIT_BYTES)


def _group(i, tm, tc, ll):
    row0 = i * tm
    return jnp.where(row0 < tc, 0, 1 + (row0 - tc) // ll)


def _norm(r):
    mu = jnp.mean(r, axis=-1, keepdims=True)
    d = r - mu
    var = jnp.mean(d * d, axis=-1, keepdims=True)
    return d * lax.rsqrt(var + LN_EPS)


def _silu(x):
    return x * jax.nn.sigmoid(x)


def _pack_halves(x):
    n2 = x.shape[1] // 2
    lo = pltpu.bitcast(x[:, :n2].astype(BF16).astype(F32), U32)
    hi = pltpu.bitcast(x[:, n2:].astype(BF16).astype(F32), U32)
    return (lo >> 16) | (hi & jnp.uint32(0xFFFF0000))


def _unpack_halves(p):
    return pltpu.bitcast(p << 16, F32), pltpu.bitcast(p & jnp.uint32(0xFFFF0000), F32)


def _mod_kernel(c_ref, w_ref, b_ref, o_ref):
    s = _silu(c_ref[...]).astype(BF16)
    o_ref[0] = jnp.dot(s, w_ref[0].astype(BF16), preferred_element_type=F32) + b_ref[0]


def _modulation(cond, ada_w, ada_b):
    L, D, N = ada_w.shape
    tn = _tile(N, 1024)
    return pl.pallas_call(
        _mod_kernel,
        grid=(L, N // tn),
        in_specs=[pl.BlockSpec((MOD_ROWS, D), lambda l, j: (0, 0)),
                  pl.BlockSpec((1, D, tn), lambda l, j: (l, 0, j)),
                  pl.BlockSpec((1, 1, tn), lambda l, j: (l, 0, j))],
        out_specs=pl.BlockSpec((1, MOD_ROWS, tn), lambda l, j: (l, 0, j)),
        out_shape=jax.ShapeDtypeStruct((L, MOD_ROWS, N), F32),
        compiler_params=_params("parallel", "parallel"),
        name="adaln_modulation",
    )(cond, ada_w, ada_b.reshape(L, 1, N))


def _mm_mod_kernel(*refs, tm, tc, ll, has_bias, gelu):
    if has_bias:
        x_ref, sh_ref, sc_ref, w_ref, b_ref, o_ref, hb_ref = refs
    else:
        x_ref, sh_ref, sc_ref, w_ref, o_ref, hb_ref = refs
    i, j = pl.program_id(0), pl.program_id(1)

    @pl.when(j == 0)
    def _():
        g = _group(i, tm, tc, ll)
        h = x_ref[...] * (1.0 + sc_ref[pl.ds(g, 1), :]) + sh_ref[pl.ds(g, 1), :]
        hb_ref[...] = h.astype(BF16)

    acc = jnp.dot(hb_ref[...], w_ref[...], preferred_element_type=F32)
    if has_bias:
        acc = acc + b_ref[...]
    if gelu:
        acc = 0.5 * acc * (1.0 + lax.erf(acc * (2.0 ** -0.5)))
    o_ref[...] = acc.astype(o_ref.dtype)


def _mm_mod(x, mod, shift_chunk, w, bias, *, tc, ll, gelu=False):
    T, D = x.shape
    N = w.shape[1]
    tm = _row_tile(tc, ll, 1024)
    tn = _tile(N, 1024)
    in_specs = [pl.BlockSpec((tm, D), lambda i, j: (i, 0)),
                pl.BlockSpec((MOD_ROWS, D), lambda i, j: (0, shift_chunk)),
                pl.BlockSpec((MOD_ROWS, D), lambda i, j: (0, shift_chunk + 1)),
                pl.BlockSpec((D, tn), lambda i, j: (0, j))]
    args = [x, mod, mod, w]
    if bias is not None:
        in_specs.append(pl.BlockSpec((1, tn), lambda i, j: (0, j)))
        args.append(bias.reshape(1, N))
    return pl.pallas_call(
        functools.partial(_mm_mod_kernel, tm=tm, tc=tc, ll=ll, has_bias=bias is not None, gelu=gelu),
        grid=(T // tm, N // tn),
        in_specs=in_specs,
        out_specs=pl.BlockSpec((tm, tn), lambda i, j: (i, j)),
        out_shape=jax.ShapeDtypeStruct((T, N), BF16),
        scratch_shapes=[pltpu.VMEM((tm, D), BF16)],
        compiler_params=_params("parallel", "arbitrary"),
        name="modulate_in_proj",
    )(*args)


def _out_ln_kernel(*refs, n_y, nk, tm, tc, ll, alpha):
    y_refs = refs[:n_y]
    w_ref, x_ref, gate_ref, lng_ref, lnb_ref, o_ref, acc_ref = refs[n_y:]
    i, k = pl.program_id(0), pl.program_id(1)

    @pl.when(k == 0)
    def _():
        acc_ref[...] = jnp.zeros_like(acc_ref)

    if n_y == 1:
        y = y_refs[0][...]
    else:
        y = (y_refs[0][...].astype(F32) + y_refs[1][...].astype(F32)).astype(BF16)
    acc_ref[...] += jnp.dot(y, w_ref[...], preferred_element_type=F32)

    @pl.when(k == nk - 1)
    def _():
        g = _group(i, tm, tc, ll)
        r = alpha * x_ref[...] + gate_ref[pl.ds(g, 1), :] * acc_ref[...]
        o_ref[...] = _norm(r) * lng_ref[...] + lnb_ref[...]


def _out_ln(ys, w, x, mod, gate_chunk, ln_g, ln_b, *, tc, ll, alpha):
    T, D = x.shape
    K = w.shape[0]
    tm = _row_tile(tc, ll, 512)
    tk = _tile(K, 2048)
    nk = K // tk
    in_specs = [pl.BlockSpec((tm, tk), lambda i, k: (i, k)) for _ in ys]
    in_specs += [pl.BlockSpec((tk, D), lambda i, k: (k, 0)),
                 pl.BlockSpec((tm, D), lambda i, k: (i, 0)),
                 pl.BlockSpec((MOD_ROWS, D), lambda i, k: (0, gate_chunk)),
                 pl.BlockSpec((1, D), lambda i, k: (0, 0)),
                 pl.BlockSpec((1, D), lambda i, k: (0, 0))]
    return pl.pallas_call(
        functools.partial(_out_ln_kernel, n_y=len(ys), nk=nk, tm=tm, tc=tc, ll=ll, alpha=alpha),
        grid=(T // tm, nk),
        in_specs=in_specs,
        out_specs=pl.BlockSpec((tm, D), lambda i, k: (i, 0)),
        out_shape=jax.ShapeDtypeStruct((T, D), F32),
        scratch_shapes=[pltpu.VMEM((tm, D), F32)],
        compiler_params=_params("parallel", "arbitrary"),
        name="out_proj_residual_ln",
    )(*ys, w, x, mod, ln_g.reshape(1, D), ln_b.reshape(1, D))


def _ret_item(n, lay):
    ncx, nc_c, nc_l = lay
    is_ctx = n < ncx
    m = jnp.where(is_ctx, n, n - ncx)
    seq = jnp.where(is_ctx, m // nc_c, m // nc_l)
    c = jnp.where(is_ctx, m % nc_c, m % nc_l)
    nc = jnp.where(is_ctx, nc_c, nc_l)
    return is_ctx, seq, c, nc, n - c


def _ret_kernel(*refs, n_prev, lay, scale):
    dec_ref, qf, kf, vf, gf, qb, kb, vb, gb, s0_ref = refs[:10]
    prev_refs = refs[10:10 + n_prev]
    yf_ref, yb_ref, out_ref, st_ref = refs[10 + n_prev:]
    hb, dk, dv = st_ref.shape[1:]
    h0 = pl.program_id(0) * hb
    is_ctx, _, c, nc, _ = _ret_item(pl.program_id(1), lay)
    C = RET_CHUNK

    @pl.when(c == 0)
    def _():
        st_ref[...] = jnp.where(is_ctx, 0.0, s0_ref[...])

    ri = lax.broadcasted_iota(I32, (C, C), 0)
    ci = lax.broadcasted_iota(I32, (C, C), 1)
    row = lax.broadcasted_iota(I32, (C, 1), 0).astype(F32)
    dirs = ((qf, kf, vf, gf, yf_ref), (qb, kb, vb, gb, yb_ref))
    for d, (q_ref, k_ref, v_ref, g_ref, y_ref) in enumerate(dirs):
        rel = ri - ci if d == 0 else ci - ri
        pos = row if d == 0 else (C - 1.0) - row
        for hh in range(hb):
            kc, vc = slice(hh * dk, (hh + 1) * dk), slice(hh * dv, (hh + 1) * dv)
            lg = -jnp.exp(jnp.full((1, 1), dec_ref[d, h0 + hh], F32))
            dmask = jnp.where(rel >= 0, jnp.exp(lg * jnp.maximum(rel, 0).astype(F32)), 0.0) * scale
            xi = jnp.exp(lg * (pos + 1.0))
            zeta = jnp.exp(lg * ((C - 1.0) - pos)) * scale
            q, k, v = q_ref[:, kc], k_ref[:, kc], v_ref[:, vc]
            s = st_ref[d, hh]
            scores = lax.dot_general(q, k, NT_DIMS, preferred_element_type=F32) * dmask
            o = (jnp.dot(scores.astype(BF16), v, preferred_element_type=F32)
                 + jnp.dot(q, s.astype(BF16), preferred_element_type=F32) * xi)
            kz = (k.astype(F32) * zeta).astype(BF16)
            st_ref[d, hh] = (jnp.exp(lg * float(C)) * s
                             + lax.dot_general(kz, v, TN_DIMS, preferred_element_type=F32))
            y_ref[:, vc] = (_silu(g_ref[:, vc].astype(F32)) * _norm(o)).astype(y_ref.dtype)

    @pl.when(jnp.logical_and(is_ctx, c == nc - 1))
    def _():
        for p, prev_ref in enumerate(prev_refs):
            out_ref[p] = prev_ref[...]
        out_ref[n_prev] = st_ref[...]


def _retention(a, decay, s0, s0_layer, prev_states, *, n_ctx_seq, ctx_len, lat_len):
    T = a.shape[0]
    H = RET_HEADS
    DK, DV = s0.shape[-2], s0.shape[-1]
    C = RET_CHUNK
    lay = (n_ctx_seq * ctx_len // C, ctx_len // C, lat_len // C)
    n_prev = 0 if prev_states is None else prev_states.shape[1]

    def fwd(n):
        return n

    def bwd(n):
        _, _, c, nc, first = _ret_item(n, lay)
        return first + (nc - 1 - c)

    def lat_seq(n):
        is_ctx, seq, _, _, _ = _ret_item(n, lay)
        return jnp.where(is_ctx, 0, seq)

    def ctx_seq(n):
        is_ctx, seq, _, _, _ = _ret_item(n, lay)
        return jnp.where(is_ctx, seq, n_ctx_seq - 1)

    hb = RET_HEADS_PER_STEP
    nh = H // hb

    def col_specs(rowf):
        return [pl.BlockSpec((C, hb * DK), lambda h, n: (rowf(n), h)),
                pl.BlockSpec((C, hb * DK), lambda h, n: (rowf(n), nh + h)),
                pl.BlockSpec((C, hb * DV), lambda h, n: (rowf(n), nh + h))]

    def state_spec(layers, seqf):
        return pl.BlockSpec((None, layers, 2, hb, DK, DV), lambda h, n: (seqf(n), 0, 0, h, 0, 0))

    in_specs = [pl.BlockSpec(memory_space=pltpu.SMEM)]
    in_specs += col_specs(fwd) + [pl.BlockSpec((C, hb * DV), lambda h, n: (fwd(n), 2 * nh + h))]
    in_specs += col_specs(bwd) + [pl.BlockSpec((C, hb * DV), lambda h, n: (bwd(n), 3 * nh + h))]
    in_specs.append(pl.BlockSpec((None, None, 2, hb, DK, DV), lambda h, n: (lat_seq(n), s0_layer, 0, h, 0, 0)))
    args = [decay] + [a] * 8 + [s0]
    if n_prev:
        in_specs.append(state_spec(n_prev, ctx_seq))
        args.append(prev_states)

    def kern(*refs):
        if n_prev:
            refs = refs[:10] + tuple(refs[10].at[p] for p in range(n_prev)) + refs[11:]
        _ret_kernel(*refs, n_prev=n_prev, lay=lay, scale=float(DK) ** -0.5)

    V = H * DV
    return pl.pallas_call(
        kern,
        grid=(nh, T // C),
        in_specs=in_specs,
        out_specs=[pl.BlockSpec((C, hb * DV), lambda h, n: (fwd(n), h)),
                   pl.BlockSpec((C, hb * DV), lambda h, n: (bwd(n), h)),
                   state_spec(n_prev + 1, ctx_seq)],
        out_shape=[jax.ShapeDtypeStruct((T, V), BF16), jax.ShapeDtypeStruct((T, V), BF16),
                   jax.ShapeDtypeStruct((n_ctx_seq, n_prev + 1, 2, H, DK, DV), F32)],
        scratch_shapes=[pltpu.VMEM((2, hb, DK, DV), F32)],
        compiler_params=_params("parallel", "arbitrary"),
        name="retention_core",
    )(*args)


def _gmlp_kernel(u_ref, v_ref, lng_ref, lnb_ref, ws_ref, bs_ref, o_ref):
    vn = (_norm(v_ref[...].astype(F32)) * lng_ref[...] + lnb_ref[...]).astype(BF16)
    gw = vn.shape[1] // GM_GROUPS
    for g in range(GM_GROUPS):
        cols = slice(g * gw, (g + 1) * gw)
        mixed = jnp.dot(ws_ref[g], vn[:, cols], preferred_element_type=F32) + bs_ref[:, g:g + 1]
        o_ref[:, cols] = (u_ref[:, cols].astype(F32) * mixed).astype(o_ref.dtype)


def _gmlp_core(z, ln_g, ln_b, w_s, b_s):
    T, W2 = z.shape
    W = W2 // 2
    C = GM_CHUNK
    return pl.pallas_call(
        _gmlp_kernel,
        grid=(T // C,),
        in_specs=[pl.BlockSpec((C, W), lambda n: (n, 0)),
                  pl.BlockSpec((C, W), lambda n: (n, 1)),
                  pl.BlockSpec((1, W), lambda n: (0, 0)),
                  pl.BlockSpec((1, W), lambda n: (0, 0)),
                  pl.BlockSpec((GM_GROUPS, C, C), lambda n: (0, 0, 0)),
                  pl.BlockSpec((C, GM_GROUPS), lambda n: (0, 0))],
        out_specs=pl.BlockSpec((C, W), lambda n: (n, 0)),
        out_shape=jax.ShapeDtypeStruct((T, W), BF16),
        compiler_params=_params("parallel"),
        name="gmlp_spatial_gate",
    )(z, z, ln_g.reshape(1, W), ln_b.reshape(1, W), w_s.astype(BF16), b_s.T)


def _sconv_kernel(b_ref, c_ref, x_ref, w_ref, o_ref, *, tm, tc, p_ctx, p_lat):
    i = pl.program_id(0)
    t = c_ref[...].astype(F32) * x_ref[...].astype(F32)
    row = lax.broadcasted_iota(I32, (tm, 1), 0)
    is_ctx = i * tm < tc
    pos = jnp.where(is_ctx, row % p_ctx, row % p_lat)
    t_prev = jnp.where(pos == 0, 0.0, pltpu.roll(t, 1, 0))
    t_next = jnp.where(pos == jnp.where(is_ctx, p_ctx - 1, p_lat - 1), 0.0, pltpu.roll(t, tm - 1, 0))
    conv = t_prev * w_ref[0:1, :] + t * w_ref[1:2, :] + t_next * w_ref[2:3, :]
    o_ref[...] = (b_ref[...].astype(F32) * conv).astype(o_ref.dtype)


def _sconv_core(a, conv_w, *, tc, ll, p_ctx, p_lat):
    T, D3 = a.shape
    D = D3 // 3
    tm = _row_tile(tc, ll, 512)
    assert tm % p_ctx == 0 and tm % p_lat == 0
    tn = _tile(D, 512)
    nj = D // tn
    return pl.pallas_call(
        functools.partial(_sconv_kernel, tm=tm, tc=tc, p_ctx=p_ctx, p_lat=p_lat),
        grid=(T // tm, nj),
        in_specs=[pl.BlockSpec((tm, tn), lambda i, j: (i, j)),
                  pl.BlockSpec((tm, tn), lambda i, j: (i, nj + j)),
                  pl.BlockSpec((tm, tn), lambda i, j: (i, 2 * nj + j)),
                  pl.BlockSpec((3, tn), lambda i, j: (0, j))],
        out_specs=pl.BlockSpec((tm, tn), lambda i, j: (i, j)),
        out_shape=jax.ShapeDtypeStruct((T, D), BF16),
        compiler_params=_params("parallel", "parallel"),
        name="short_conv_gate",
    )(a, a, a, conv_w)


def _router_kernel(x_ref, sh_ref, sc_ref, wt_ref, b_ref, hp_ref, idx_ref, gate_ref, rank_ref, cnt_ref,
                   carry_ref, *, tm, tc, ll):
    i = pl.program_id(0)
    E = N_EXPERTS

    @pl.when(i == 0)
    def _():
        carry_ref[...] = jnp.zeros_like(carry_ref)

    g = _group(i, tm, tc, ll)
    h = x_ref[...] * (1.0 + sc_ref[pl.ds(g, 1), :]) + sh_ref[pl.ds(g, 1), :]
    hp_ref[...] = _pack_halves(h)
    hh = h.astype(BF16)
    hl = (h - hh.astype(F32)).astype(BF16)
    w = wt_ref[...]
    wh = w.astype(BF16)
    wl = (w - wh.astype(F32)).astype(BF16)
    logits = (lax.dot_general(wh, hh, NT_DIMS, preferred_element_type=F32)
              + lax.dot_general(wh, hl, NT_DIMS, preferred_element_type=F32)
              + lax.dot_general(wl, hh, NT_DIMS, preferred_element_type=F32)) + b_ref[...]
    e_iota = lax.broadcasted_iota(I32, (E, tm), 0)
    earlier = (lax.broadcasted_iota(I32, (tm, tm), 0) < lax.broadcasted_iota(I32, (tm, tm), 1)).astype(BF16)
    count = carry_ref[...]
    vals = logits
    tops = []
    for k in range(TOP_K):
        m = jnp.max(vals, axis=0, keepdims=True)
        idx = jnp.min(jnp.where(vals == m, e_iota, E), axis=0, keepdims=True)
        hit = e_iota == idx
        onehot = jnp.where(hit, 1.0, 0.0)
        before = jnp.dot(onehot.astype(BF16), earlier, preferred_element_type=F32)
        rank = jnp.sum(onehot * (count + before), axis=0, keepdims=True)
        count = count + jnp.sum(onehot, axis=1, keepdims=True)
        idx_ref[k:k + 1, :] = idx
        rank_ref[k:k + 1, :] = rank.astype(I32)
        tops.append(m)
        vals = jnp.where(hit, -jnp.inf, vals)
    carry_ref[...] = count
    cnt_ref[...] = count
    exps = [jnp.exp(t - tops[0]) for t in tops]
    den = exps[0]
    for e in exps[1:]:
        den = den + e
    for k in range(TOP_K):
        gate_ref[k:k + 1, :] = exps[k] / den


def _router(x, mod, shift_chunk, router_w, router_b, *, tc, ll):
    T, D = x.shape
    E = N_EXPERTS
    tm = _row_tile(tc, ll, 256)
    row = lambda i: (0, i)
    return pl.pallas_call(
        functools.partial(_router_kernel, tm=tm, tc=tc, ll=ll),
        grid=(T // tm,),
        in_specs=[pl.BlockSpec((tm, D), lambda i: (i, 0)),
                  pl.BlockSpec((MOD_ROWS, D), lambda i: (0, shift_chunk)),
                  pl.BlockSpec((MOD_ROWS, D), lambda i: (0, shift_chunk + 1)),
                  pl.BlockSpec((E, D), lambda i: (0, 0)),
                  pl.BlockSpec((E, 1), lambda i: (0, 0))],
        out_specs=[pl.BlockSpec((tm, D // 2), lambda i: (i, 0)),
                   pl.BlockSpec((TOP_K, tm), row),
                   pl.BlockSpec((TOP_K, tm), row),
                   pl.BlockSpec((TOP_K, tm), row),
                   pl.BlockSpec((E, 1), lambda i: (0, 0))],
        out_shape=[jax.ShapeDtypeStruct((T, D // 2), U32),
                   jax.ShapeDtypeStruct((TOP_K, T), I32),
                   jax.ShapeDtypeStruct((TOP_K, T), F32),
                   jax.ShapeDtypeStruct((TOP_K, T), I32),
                   jax.ShapeDtypeStruct((E, 1), F32)],
        scratch_shapes=[pltpu.VMEM((E, 1), F32)],
        compiler_params=_params("arbitrary"),
        name="moe_router",
    )(x, mod, mod, router_w.T, router_b.reshape(E, 1))


def _expert_changed(be_ref, i):
    return jnp.logical_or(i == 0, be_ref[i] != be_ref[jnp.maximum(i - 1, 0)])


def _dispatch_kernel(lo_ref, hi_ref, dest_ref, h_ref, xs_ref, zero_ref, sem, *, n_pad):
    tm = h_ref.shape[0]

    @pl.when(pl.program_id(0) == 0)
    def _():
        zero_ref[...] = jnp.zeros_like(zero_ref)

        def zero_row(r, carry):
            pltpu.make_async_copy(zero_ref.at[pl.ds(0, 1)], xs_ref.at[pl.ds(r, 1)], sem).start()
            return carry

        def zero_expert(e, carry):
            return lax.fori_loop(lo_ref[e], hi_ref[e], zero_row, carry)

        lax.fori_loop(0, N_EXPERTS, zero_expert, 0)
        for _ in range(n_pad // tm):
            pltpu.make_async_copy(h_ref, xs_ref.at[pl.ds(0, tm)], sem).wait()

    def issue(r, carry):
        for k in range(TOP_K):
            pltpu.make_async_copy(h_ref.at[pl.ds(r, 1)], xs_ref.at[pl.ds(dest_ref[k, r], 1)], sem).start()
        return carry

    lax.fori_loop(0, tm, issue, 0)
    for k in range(TOP_K):
        pltpu.make_async_copy(h_ref, xs_ref.at[pl.ds(0, tm)], sem).wait()


def _dispatch(hp, dest, pad_lo, pad_hi, n_rows):
    T, half = hp.shape
    tm = _tile(T, 256)
    n_pad = n_rows - T * TOP_K
    assert n_pad % tm == 0
    return pl.pallas_call(
        functools.partial(_dispatch_kernel, n_pad=n_pad),
        grid_spec=pltpu.PrefetchScalarGridSpec(
            num_scalar_prefetch=2,
            grid=(T // tm,),
            in_specs=[pl.BlockSpec((TOP_K, tm), lambda i, lo, hi: (0, i), memory_space=pltpu.SMEM),
                      pl.BlockSpec((tm, half), lambda i, lo, hi: (i, 0))],
            out_specs=pl.BlockSpec(memory_space=pl.ANY),
            scratch_shapes=[pltpu.VMEM((8, half), hp.dtype), pltpu.SemaphoreType.DMA(())]),
        out_shape=jax.ShapeDtypeStruct((n_rows, half), hp.dtype),
        compiler_params=_params("arbitrary"),
        name="moe_dispatch",
    )(pad_lo, pad_hi, dest, hp)


def _gate_up_kernel(be_ref, nu_ref, x_ref, wg_ref, wu_ref, bg_ref, bu_ref, o_ref, wgb_ref, wub_ref):
    i = pl.program_id(1)

    @pl.when(_expert_changed(be_ref, i))
    def _():
        wgb_ref[...] = wg_ref[...].astype(BF16)
        wub_ref[...] = wu_ref[...].astype(BF16)

    @pl.when(i < nu_ref[0])
    def _():
        half = x_ref.shape[1]
        lo, hi = _unpack_halves(x_ref[...])
        lo, hi = lo.astype(BF16), hi.astype(BF16)

        def proj(w_ref, b_ref):
            return (jnp.dot(lo, w_ref[:half, :], preferred_element_type=F32)
                    + jnp.dot(hi, w_ref[half:, :], preferred_element_type=F32) + b_ref[...])

        gate = jnp.minimum(proj(wgb_ref, bg_ref), SWIGLU_LIMIT)
        up = jnp.clip(proj(wub_ref, bu_ref), -SWIGLU_LIMIT, SWIGLU_LIMIT)
        act = (up + 1.0) * (gate * jax.nn.sigmoid(SWIGLU_ALPHA * gate))
        o_ref[...] = act.astype(o_ref.dtype)

    @pl.when(i >= nu_ref[0])
    def _():
        o_ref[...] = jnp.zeros_like(o_ref)


def _gate_up(xs, blk_expert, n_used, w_gu, b_gu, layer):
    R, half = xs.shape
    tm = MOE_ROWS
    nb = R // tm
    D = 2 * half
    F = w_gu.shape[-1] // 2
    tf = _tile(F, 1024)
    nf = F // tf
    last = lambda i, nu: jnp.minimum(i, nu[0] - 1)
    return pl.pallas_call(
        _gate_up_kernel,
        grid_spec=pltpu.PrefetchScalarGridSpec(
            num_scalar_prefetch=2,
            grid=(nf, nb),
            in_specs=[pl.BlockSpec((tm, half), lambda j, i, be, nu: (last(i, nu), 0)),
                      pl.BlockSpec((None, None, D, tf), lambda j, i, be, nu: (layer, be[i], 0, j)),
                      pl.BlockSpec((None, None, D, tf), lambda j, i, be, nu: (layer, be[i], 0, nf + j)),
                      pl.BlockSpec((None, None, 1, tf), lambda j, i, be, nu: (layer, be[i], 0, j)),
                      pl.BlockSpec((None, None, 1, tf), lambda j, i, be, nu: (layer, be[i], 0, nf + j))],
            out_specs=pl.BlockSpec((tm, tf), lambda j, i, be, nu: (i, j)),
            scratch_shapes=[pltpu.VMEM((D, tf), BF16), pltpu.VMEM((D, tf), BF16)]),
        out_shape=jax.ShapeDtypeStruct((R, F), BF16),
        compiler_params=_params("arbitrary", "arbitrary"),
        name="moe_gate_up",
    )(blk_expert, n_used, xs, w_gu, w_gu, b_gu, b_gu)


def _down_kernel(be_ref, nu_ref, slotp_ref, slotc_ref, a_ref, w_ref, b_ref, yt_ref, wb_ref, ya_ref, yb_ref,
                 sems):
    i = pl.program_id(0)
    nu = nu_ref[0]
    nb = pl.num_programs(0)
    tm = ya_ref.shape[0]
    n_slots = yt_ref.shape[0]

    def scatter(slot_ref, y_ref, sem):
        for r in range(tm):
            pltpu.make_async_copy(y_ref.at[pl.ds(r, 1)], yt_ref.at[pl.ds(slot_ref[0, r], 1)], sem).start()

    def wait_rows(y_ref, sem):
        pltpu.make_async_copy(y_ref, yt_ref.at[pl.ds(0, tm)], sem).wait()

    @pl.when(_expert_changed(be_ref, i))
    def _():
        wb_ref[...] = w_ref[...].astype(BF16)

    @pl.when(i == 0)
    def _():
        ya_ref[...] = jnp.zeros_like(ya_ref)
        yb_ref[...] = jnp.zeros_like(yb_ref)
        pltpu.make_async_copy(ya_ref, yt_ref.at[pl.ds(n_slots - tm, tm)], sems.at[0]).start()

    def step(y_ref, sem, py_ref, psem, used):
        wait_rows(y_ref, sem)
        if used:
            a = a_ref[...]
        scatter(slotp_ref, py_ref, psem)
        if used:
            y = jnp.dot(a, wb_ref[...], preferred_element_type=F32) + b_ref[...]
            y_ref[...] = _pack_halves(y)
        else:
            y_ref[...] = jnp.zeros_like(y_ref)

        @pl.when(i == nb - 1)
        def _():
            scatter(slotc_ref, y_ref, sem)
            wait_rows(y_ref, sem)
            wait_rows(py_ref, psem)

    for parity, (y_ref, py_ref) in enumerate(((ya_ref, yb_ref), (yb_ref, ya_ref))):
        for used in (True, False):
            @pl.when(jnp.logical_and(lax.rem(i, 2) == parity, (i < nu) == used))
            def _(y_ref=y_ref, py_ref=py_ref, parity=parity, used=used):
                step(y_ref, sems.at[parity], py_ref, sems.at[1 - parity], used)


def _down(act, slot, blk_expert, n_used, w_down, b_down, layer):
    R, F = act.shape
    _, _, tm = slot.shape
    nb = R // tm
    n_slots = R + 2 * tm
    D = w_down.shape[-1]
    last = lambda i, nu: jnp.minimum(i, nu[0] - 1)
    smem_blk = lambda f: pl.BlockSpec((None, 1, tm), f, memory_space=pltpu.SMEM)
    return pl.pallas_call(
        _down_kernel,
        grid_spec=pltpu.PrefetchScalarGridSpec(
            num_scalar_prefetch=2,
            grid=(nb,),
            in_specs=[smem_blk(lambda i, be, nu: (i, 0, 0)),
                      smem_blk(lambda i, be, nu: (i + 1, 0, 0)),
                      pl.BlockSpec((tm, F), lambda i, be, nu: (last(i, nu), 0)),
                      pl.BlockSpec((None, None, F, D), lambda i, be, nu: (layer, be[i], 0, 0)),
                      pl.BlockSpec((None, None, 1, D), lambda i, be, nu: (layer, be[i], 0, 0))],
            out_specs=pl.BlockSpec(memory_space=pl.ANY),
            scratch_shapes=[pltpu.VMEM((F, D), BF16),
                            pltpu.VMEM((tm, D // 2), U32), pltpu.VMEM((tm, D // 2), U32),
                            pltpu.SemaphoreType.DMA((2,))]),
        out_shape=jax.ShapeDtypeStruct((n_slots, D // 2), U32),
        compiler_params=_params("arbitrary"),
        name="moe_down",
    )(blk_expert, n_used, slot, slot, act, w_down, b_down)


def _combine_kernel(*refs, tm, tc, ll, alpha):
    y_refs = refs[:TOP_K]
    gt_ref, x_ref, gate_ref, lng_ref, lnb_ref, *o_refs = refs[TOP_K:]
    i = pl.program_id(0)
    f_lo = f_hi = None
    for k, y_ref in enumerate(y_refs):
        lo, hi = _unpack_halves(y_ref[...])
        w = gt_ref[:, k:k + 1]
        f_lo = w * lo if k == 0 else f_lo + w * lo
        f_hi = w * hi if k == 0 else f_hi + w * hi
    f = jnp.concatenate([f_lo, f_hi], axis=1)
    g = _group(i, tm, tc, ll)
    r = alpha * x_ref[...] + gate_ref[pl.ds(g, 1), :] * f
    out = _norm(r) * lng_ref[...] + lnb_ref[...]
    if len(o_refs) == 1:
        o_refs[0][...] = out
    else:
        @pl.when(i * tm < tc)
        def _():
            o_refs[0][...] = out

        @pl.when(i * tm >= tc)
        def _():
            o_refs[1][...] = out


def _combine(yt, gates_t, x, mod, gate_chunk, ln_g, ln_b, *, tc, ll, alpha, split):
    T, D = x.shape
    tm = _row_tile(tc, ll, 256)
    nbc = tc // tm
    nbt = T // tm
    if split:
        out_specs = [pl.BlockSpec((tm, D), lambda i: (jnp.minimum(i, nbc - 1), 0)),
                     pl.BlockSpec((tm, D), lambda i: (jnp.maximum(i - nbc, 0), 0))]
        out_shape = [jax.ShapeDtypeStruct((tc, D), F32), jax.ShapeDtypeStruct((T - tc, D), F32)]
    else:
        out_specs = pl.BlockSpec((tm, D), lambda i: (i, 0))
        out_shape = jax.ShapeDtypeStruct((T, D), F32)
    return pl.pallas_call(
        functools.partial(_combine_kernel, tm=tm, tc=tc, ll=ll, alpha=alpha),
        grid=(T // tm,),
        in_specs=[pl.BlockSpec((tm, D // 2), functools.partial(lambda k, i: (k * nbt + i, 0), k))
                  for k in range(TOP_K)]
        + [pl.BlockSpec((tm, TOP_K), lambda i: (i, 0)),
           pl.BlockSpec((tm, D), lambda i: (i, 0)),
           pl.BlockSpec((MOD_ROWS, D), lambda i: (0, gate_chunk)),
           pl.BlockSpec((1, D), lambda i: (0, 0)),
           pl.BlockSpec((1, D), lambda i: (0, 0))],
        out_specs=out_specs,
        out_shape=out_shape,
        compiler_params=_params("arbitrary"),
        name="moe_combine_residual_ln",
    )(*[yt] * TOP_K, gates_t, x, mod, ln_g.reshape(1, D), ln_b.reshape(1, D))


def _moe_block(x, mod, layer, router_w, router_b, w_gu, b_gu, w_down, b_down, ln_g, ln_b, *, tc, ll, alpha,
               split):
    T, D = x.shape
    E = N_EXPERTS
    A = T * TOP_K
    tm = MOE_ROWS
    nb = pl.cdiv(A + E * (tm - 1), tm)
    n_rows = nb * tm
    big = pl.next_power_of_2(n_rows)
    assert (E + 1) * big < 2 ** 31
    hp, idx, gates, rank, cnt = _router(x, mod, 3, router_w, router_b, tc=tc, ll=ll)
    counts = cnt[:, 0].astype(I32)
    padded = (counts + tm - 1) // tm * tm
    pad_ends = jnp.cumsum(padded)
    pad_starts = pad_ends - padded
    blk_start = jnp.arange(nb, dtype=I32) * tm
    blk_expert = jnp.minimum(jnp.sum(blk_start[:, None] >= pad_ends[None, :], axis=1), E - 1).astype(I32)
    n_used = (pad_ends[-1:] // tm).astype(I32)
    onehot = idx[:, :, None] == jnp.arange(E, dtype=I32)
    dest = jnp.sum(jnp.where(onehot, pad_starts, 0), axis=-1) + rank
    tr = _row_tile(tc, ll, 256)
    k_ids, t_ids = jnp.arange(TOP_K, dtype=I32)[:, None], jnp.arange(T, dtype=I32)[None, :]
    order = (t_ids // tr) * (TOP_K * tr) + k_ids * tr + t_ids % tr
    f_ids = jnp.arange(n_rows - A, dtype=I32)
    f_expert = jnp.sum(f_ids[:, None] >= jnp.cumsum(padded - counts)[None, :], axis=1).astype(I32)
    keys = jnp.concatenate([(idx * big + order).reshape(A), f_expert * big + A + f_ids])
    row_o = jnp.sort(keys) & (big - 1)
    row_a = (row_o // tr % TOP_K) * T + row_o // (TOP_K * tr) * tr + row_o % tr
    slot = jnp.concatenate([n_rows + jnp.arange(tm, dtype=I32), jnp.where(row_o < A, row_a, row_o)])
    slot = slot.reshape(nb + 1, 1, tm)
    pad_hi = pad_ends.at[E - 1].set(n_rows).astype(I32)
    E4 = b_gu.shape
    xs = _dispatch(hp, dest, (pad_starts + counts).astype(I32), pad_hi, n_rows)
    act = _gate_up(xs, blk_expert, n_used, w_gu, b_gu.reshape(E4[0], E4[1], 1, E4[2]), layer)
    yt = _down(act, slot, blk_expert, n_used, w_down, b_down.reshape(b_down.shape[0], E, 1, D), layer)
    return _combine(yt, gates.T, x, mod, 5, ln_g, ln_b, tc=tc, ll=ll, alpha=alpha, split=split)


def kernel(x_prompt, x_sample, state_ret, c, c_ctx, ada_w, ada_b, ln_g, ln_b, ret_w_in, ret_w_out, ret_decay,
           gm_w_in, gm_b_in, gm_ln_g, gm_ln_b, gm_w_s, gm_b_s, gm_w_out, sc_w_in, sc_conv, sc_w_out,
           moe_router_w, moe_router_b, moe_w_gu, moe_b_gu, moe_w_down, moe_b_down):
    B, S, D = x_prompt.shape
    BL, SL, _ = x_sample.shape
    depth = ada_w.shape[0]
    assert 1 + BL <= MOD_ROWS
    assert S % RET_CHUNK == 0 and SL % RET_CHUNK == 0 and SL % GRID_W == 0
    tc, ll = B * S, SL
    alpha = (2.0 * depth) ** 0.25

    x = jnp.concatenate([x_prompt.reshape(tc, D), x_sample.reshape(BL * SL, D)], axis=0)
    cond = jnp.concatenate([c_ctx[None], c, jnp.zeros((MOD_ROWS - 1 - BL, D), F32)], axis=0)
    mods = _modulation(cond, ada_w, ada_b)
    states = None
    kw = dict(tc=tc, ll=ll)
    for i in range(depth):
        kind, j = i % N_MIXERS, i // N_MIXERS
        mod = mods[i]
        if kind == 0:
            a = _mm_mod(x, mod, 0, ret_w_in[j].astype(BF16), None, **kw)
            yf, yb, states = _retention(a, ret_decay[j], state_ret, j, states, n_ctx_seq=B, ctx_len=S, lat_len=SL)
            ys, w_out = [yf, yb], ret_w_out[j]
        elif kind == 1:
            z = _mm_mod(x, mod, 0, gm_w_in[j].astype(BF16), gm_b_in[j], gelu=True, **kw)
            ys, w_out = [_gmlp_core(z, gm_ln_g[j], gm_ln_b[j], gm_w_s[j], gm_b_s[j])], gm_w_out[j]
        else:
            a = _mm_mod(x, mod, 0, sc_w_in[j].astype(BF16), None, **kw)
            ys, w_out = [_sconv_core(a, sc_conv[j], p_ctx=S, p_lat=GRID_W, **kw)], sc_w_out[j]
        x = _out_ln(ys, w_out.astype(BF16), x, mod, 2, ln_g[i, 0], ln_b[i, 0], alpha=alpha, **kw)
        x = _moe_block(x, mod, i, moe_router_w[i], moe_router_b[i], moe_w_gu, moe_b_gu, moe_w_down, moe_b_down,
                       ln_g[i, 1], ln_b[i, 1], alpha=alpha, split=i == depth - 1, **kw)
    x_ctx, x_lat = x
    return (x_ctx.reshape(B, S, D), x_lat.reshape(BL, SL, D), states)
```
